```python
import jax, jax.numpy as jnp
from jax import lax
import numpy as np

D_MODEL = 2048
BATCH = 2
SEQ = 4096
DEPTH = 4

N_MIXERS = 4
RMS_EPS = 1e-6
ROPE_BASE = 10000.0
QBLOCK = 128
CHUNK = 128
POS_OFFSET_MAX = 1024

MLA_HEADS = 16
MLA_Q_RANK = 512
MLA_KV_RANK = 512
MLA_NOPE = 128
MLA_ROPE = 64
MLA_V = 128
SB_HEADS = 16
SB_HEAD_DIM = D_MODEL // SB_HEADS
SSD_D_INNER = 2 * D_MODEL
SSD_HEAD_DIM = 64
SSD_HEADS = SSD_D_INNER // SSD_HEAD_DIM
SSD_GROUPS = 8
SSD_STATE = 128
SSD_CONV = 4
SSD_CONV_CH = SSD_D_INNER + 2 * SSD_GROUPS * SSD_STATE
SSD_IN_DIM = SSD_D_INNER + SSD_CONV_CH + SSD_HEADS
RET_HEADS = 8
RET_QK_DIM = D_MODEL // RET_HEADS
RET_V_DIM = 2 * D_MODEL // RET_HEADS
RET_IN_DIM = 2 * RET_HEADS * RET_QK_DIM + 2 * RET_HEADS * RET_V_DIM
FFN_HIDDEN = 5632
FFN_CONV = 3

kernel_name = "hybrid_interleaved_mla_sb_ssd_ret"


def _n_layers_of(m):
    return len(range(m, DEPTH, N_MIXERS))


def _rms_unit(x):
    xf = x.astype(jnp.float32)
    return xf * lax.rsqrt(jnp.mean(xf * xf, axis=-1, keepdims=True) + RMS_EPS)


def _rmsnorm(x, w):
    return (_rms_unit(x) * w.astype(jnp.float32)).astype(x.dtype)


def _rope(x, positions):
    d = x.shape[-1]
    half = d // 2
    inv_freq = 1.0 / (ROPE_BASE ** (jnp.arange(half, dtype=jnp.float32) * (2.0 / d)))
    ang = positions.astype(jnp.float32)[..., None] * inv_freq
    cos = jnp.cos(ang)[:, :, None, :].astype(x.dtype)
    sin = jnp.sin(ang)[:, :, None, :].astype(x.dtype)
    x1, x2 = x[..., :half], x[..., half:]
    return jnp.concatenate([x1 * cos - x2 * sin, x2 * cos + x1 * sin], axis=-1)


def _causal_dwconv(x, w, b):
    K, C = w.shape
    y = lax.conv_general_dilated(
        x, w.astype(x.dtype)[:, None, :], window_strides=(1,), padding=[(K - 1, 0)],
        dimension_numbers=('NWC', 'WIO', 'NWC'), feature_group_count=C)
    return y + b.astype(x.dtype)


def _query_blocks(block_fn, q):
    B, S, H, Dh = q.shape
    nb = S // QBLOCK
    q_blocks = q.reshape(B, nb, QBLOCK, H, Dh).swapaxes(0, 1)
    starts = jnp.arange(nb, dtype=jnp.int32) * QBLOCK
    out = lax.map(lambda args: block_fn(args[0], args[1]), (q_blocks, starts))
    return out.swapaxes(0, 1).reshape(B, S, H, out.shape[-1])


def _exclusive_scan(states, decay):
    states = states.astype(jnp.float32)
    decay = decay.astype(jnp.float32)

    def step(h, inp):
        s, d = inp
        return d * h + s, h

    _, prev = lax.scan(step, jnp.zeros_like(states[0]), (states, decay))
    return prev


def _mla(h, positions, wq_a, q_norm, wq_b, wkv_a, kv_norm, wkv_b, wo):
    B, S, _ = h.shape
    cq = _rmsnorm(h @ wq_a, q_norm)
    q = (cq @ wq_b).reshape(B, S, MLA_HEADS, MLA_NOPE + MLA_ROPE)
    q = jnp.concatenate([q[..., :MLA_NOPE], _rope(q[..., MLA_NOPE:], positions)], axis=-1)
    kv_a = h @ wkv_a
    c_kv = _rmsnorm(kv_a[..., :MLA_KV_RANK], kv_norm)
    k_pe = _rope(kv_a[..., MLA_KV_RANK:][:, :, None, :], positions)
    kv = (c_kv @ wkv_b).reshape(B, S, MLA_HEADS, MLA_NOPE + MLA_V)
    k = jnp.concatenate(
        [kv[..., :MLA_NOPE], jnp.broadcast_to(k_pe, (B, S, MLA_HEADS, MLA_ROPE)).astype(kv.dtype)], axis=-1)
    v = kv[..., MLA_NOPE:]
    scale = (MLA_NOPE + MLA_ROPE) ** -0.5
    k_idx = jnp.arange(S, dtype=jnp.int32)

    def block(qb, t0):
        s = jnp.einsum('bqhd,bkhd->bhqk', qb, k).astype(jnp.float32) * scale
        q_idx = t0 + jnp.arange(QBLOCK, dtype=jnp.int32)
        causal = k_idx[None, :] <= q_idx[:, None]
        p = jax.nn.softmax(jnp.where(causal, s, -jnp.inf), axis=-1)
        return jnp.einsum('bhqk,bkhd->bqhd', p.astype(v.dtype), v)

    o = _query_blocks(block, q)
    return o.reshape(B, S, MLA_HEADS * MLA_V) @ wo


def _stick_breaking(h, wqkv, wo):
    B, S, _ = h.shape
    qkv = (h @ wqkv).reshape(B, S, 3, SB_HEADS, SB_HEAD_DIM)
    q, k, v = qkv[:, :, 0], qkv[:, :, 1], qkv[:, :, 2]
    scale = SB_HEAD_DIM ** -0.5
    k_idx = jnp.arange(S, dtype=jnp.int32)

    def block(qb, t0):
        z = jnp.einsum('bqhd,bkhd->bhqk', qb, k).astype(jnp.float32) * scale
        q_idx = t0 + jnp.arange(QBLOCK, dtype=jnp.int32)
        strict = k_idx[None, :] < q_idx[:, None]
        log_keep = jnp.where(strict, jax.nn.log_sigmoid(-z), 0.0)
        later = lax.cumsum(log_keep, axis=3, reverse=True) - log_keep
        a = jnp.where(strict, jnp.exp(jax.nn.log_sigmoid(z) + later), 0.0)
        return jnp.einsum('bhqk,bkhd->bqhd', a.astype(v.dtype), v)

    o = _query_blocks(block, q)
    return o.reshape(B, S, SB_HEADS * SB_HEAD_DIM) @ wo


def _ssd_scan(x, a, bm, cm):
    B, S, H, P = x.shape
    G, N = bm.shape[2], bm.shape[3]
    R = H // G
    C, L = S // CHUNK, CHUNK
    x = x.reshape(B, C, L, G, R, P)
    a = a.astype(jnp.float32).reshape(B, C, L, G, R)
    bm = bm.reshape(B, C, L, G, N)
    cm = cm.reshape(B, C, L, G, N)
    acum = jnp.cumsum(a, axis=2)
    causal = (jnp.arange(L)[:, None] >= jnp.arange(L)[None, :])[None, None, :, :, None, None]
    seg = acum[:, :, :, None] - acum[:, :, None, :]
    decay_in = jnp.exp(jnp.where(causal, seg, -jnp.inf))
    cb = jnp.einsum('bclgn,bcsgn->bclsg', cm, bm)
    y_diag = jnp.einsum('bclsgr,bcsgrp->bclgrp', cb[..., None] * decay_in, x)
    decay_out = jnp.exp(acum[:, :, -1:] - acum)
    states = jnp.einsum('bclgn,bclgrp->bcgrpn', bm, decay_out[..., None] * x)
    chunk_decay = jnp.exp(acum[:, :, -1])
    prev = _exclusive_scan(states.swapaxes(0, 1), chunk_decay.swapaxes(0, 1)[..., None, None]).swapaxes(0, 1)
    y_off = jnp.einsum('bclgn,bcgrpn->bclgrp', cm, prev) * jnp.exp(acum)[..., None]
    return (y_diag + y_off).reshape(B, S, H, P)


def _mamba2(h, w_in, conv_w, conv_b, dt_bias, a_log, d_skip, norm_w, w_out):
    B, S, _ = h.shape
    GN = SSD_GROUPS * SSD_STATE
    zxbcdt = h @ w_in
    z = zxbcdt[..., :SSD_D_INNER]
    xbc = zxbcdt[..., SSD_D_INNER:SSD_D_INNER + SSD_CONV_CH]
    dt = zxbcdt[..., SSD_D_INNER + SSD_CONV_CH:]
    xbc = jax.nn.silu(_causal_dwconv(xbc, conv_w, conv_b))
    xs = xbc[..., :SSD_D_INNER].reshape(B, S, SSD_HEADS, SSD_HEAD_DIM)
    bm = xbc[..., SSD_D_INNER:SSD_D_INNER + GN].reshape(B, S, SSD_GROUPS, SSD_STATE)
    cm = xbc[..., SSD_D_INNER + GN:].reshape(B, S, SSD_GROUPS, SSD_STATE)
    dt = jax.nn.softplus((dt + dt_bias).astype(jnp.float32))
    a_cont = -jnp.exp(a_log.astype(jnp.float32))
    y = _ssd_scan(xs * dt[..., None].astype(xs.dtype), dt * a_cont, bm, cm)
    y = y + d_skip[:, None] * xs
    y = y.reshape(B, S, SSD_D_INNER) * jax.nn.silu(z)
    y = _rmsnorm(y.reshape(B, S, SSD_GROUPS, -1), norm_w.reshape(SSD_GROUPS, -1)).reshape(B, S, SSD_D_INNER)
    return y @ w_out


def _retention(h, positions, w_in, wo):
    B, S, _ = h.shape
    H, DK, DV = RET_HEADS, RET_QK_DIM, RET_V_DIM
    C, L = S // CHUNK, CHUNK
    p = h @ w_in
    q = p[..., :H * DK].reshape(B, S, H, DK)
    k = p[..., H * DK:2 * H * DK].reshape(B, S, H, DK)
    v = p[..., 2 * H * DK:2 * H * DK + H * DV].reshape(B, S, H, DV)
    g = p[..., 2 * H * DK + H * DV:]
    q = _rope(q, positions)
    k = _rope(k, positions) * (DK ** -0.5)
    log_g = jnp.log(1.0 - 2.0 ** (-5.0 - jnp.arange(H, dtype=jnp.float32)))
    qc = q.reshape(B, C, L, H, DK)
    kc = k.reshape(B, C, L, H, DK)
    vc = v.reshape(B, C, L, H, DV)
    idx = jnp.arange(L, dtype=jnp.float32)
    diff = idx[:, None] - idx[None, :]
    d_intra = jnp.where(diff[None] >= 0, jnp.exp(jnp.maximum(diff, 0.0)[None] * log_g[:, None, None]), 0.0)
    scores = jnp.einsum('bclhd,bcshd->bchls', qc, kc) * d_intra
    inner = jnp.einsum('bchls,bcshe->bclhe', scores, vc)
    k_dec = jnp.exp((L - 1 - idx)[:, None] * log_g[None, :])
    states = jnp.einsum('bcshd,bcshe->bchde', kc * k_dec[..., None], vc)
    chunk_decay = jnp.broadcast_to(jnp.exp(L * log_g)[None, None, :, None, None], (C, 1, H, 1, 1))
    prev = _exclusive_scan(states.swapaxes(0, 1), chunk_decay).swapaxes(0, 1)
    q_dec = jnp.exp((idx + 1.0)[:, None] * log_g[None, :])
    cross = jnp.einsum('bclhd,bchde->bclhe', qc * q_dec[..., None], prev)
    o = (inner + cross).reshape(B, S, H, DV)
    o = _rms_unit(o).astype(h.dtype).reshape(B, S, H * DV)
    return (jax.nn.silu(g) * o) @ wo


def _conv_ffn(h, w_up, conv_w, conv_b, w_down):
    u = _causal_dwconv(h @ w_up, conv_w, conv_b)
    gate, up = u[..., :FFN_HIDDEN], u[..., FFN_HIDDEN:]
    return (jax.nn.silu(gate) * up) @ w_down


def setup_inputs(seed: int = 0) -> dict:
    key = jax.random.key(seed)
    k = jax.random.split(key, 32)
    f32 = jnp.float32
    D = D_MODEL
    nA, nB, nC, nD = (_n_layers_of(m) for m in range(N_MIXERS))

    def dense(kk, shape, fan_in):
        return jax.random.normal(kk, shape, f32) * (fan_in ** -0.5)

    def gain(kk, shape):
        return 1.0 + 0.02 * jax.random.normal(kk, shape, f32)

    def small(kk, shape):
        return 0.02 * jax.random.normal(kk, shape, f32)

    x = jax.random.normal(k[0], (BATCH, SEQ, D), f32)
    positions = (jnp.arange(SEQ, dtype=jnp.int32)[None, :]
                 + jax.random.randint(k[1], (BATCH, 1), 0, POS_OFFSET_MAX, dtype=jnp.int32))
    dt0 = jnp.exp(jax.random.uniform(k[17], (nC, SSD_HEADS), f32, np.log(1e-3), np.log(1e-1)))
    return {
        'x': x,
        'positions': positions,
        'norm_mix': gain(k[2], (DEPTH, D)),
        'norm_ffn': gain(k[3], (DEPTH, D)),
        'norm_final': gain(k[4], (D,)),
        'mla_wq_a': dense(k[5], (nA, D, MLA_Q_RANK), D),
        'mla_q_norm': gain(k[6], (nA, MLA_Q_RANK)),
        'mla_wq_b': dense(k[7], (nA, MLA_Q_RANK, MLA_HEADS * (MLA_NOPE + MLA_ROPE)), MLA_Q_RANK),
        'mla_wkv_a': dense(k[8], (nA, D, MLA_KV_RANK + MLA_ROPE), D),
        'mla_kv_norm': gain(k[9], (nA, MLA_KV_RANK)),
        'mla_wkv_b': dense(k[10], (nA, MLA_KV_RANK, MLA_HEADS * (MLA_NOPE + MLA_V)), MLA_KV_RANK),
        'mla_wo': dense(k[11], (nA, MLA_HEADS * MLA_V, D), MLA_HEADS * MLA_V),
        'sb_wqkv': dense(k[12], (nB, D, 3 * SB_HEADS * SB_HEAD_DIM), D),
        'sb_wo': dense(k[13], (nB, SB_HEADS * SB_HEAD_DIM, D), SB_HEADS * SB_HEAD_DIM),
        'ssd_w_in': dense(k[14], (nC, D, SSD_IN_DIM), D),
        'ssd_conv_w': dense(k[15], (nC, SSD_CONV, SSD_CONV_CH), SSD_CONV),
        'ssd_conv_b': small(k[16], (nC, SSD_CONV_CH)),
        'ssd_dt_bias': dt0 + jnp.log(-jnp.expm1(-dt0)),
        'ssd_a_log': jnp.log(jax.random.uniform(k[18], (nC, SSD_HEADS), f32, 1.0, 16.0)),
        'ssd_d': 1.0 + 0.1 * jax.random.normal(k[19], (nC, SSD_HEADS), f32),
        'ssd_norm': gain(k[20], (nC, SSD_D_INNER)),
        'ssd_w_out': dense(k[21], (nC, SSD_D_INNER, D), SSD_D_INNER),
        'ret_w_in': dense(k[22], (nD, D, RET_IN_DIM), D),
        'ret_wo': dense(k[23], (nD, RET_HEADS * RET_V_DIM, D), RET_HEADS * RET_V_DIM),
        'ffn_w_up': dense(k[24], (DEPTH, D, 2 * FFN_HIDDEN), D),
        'ffn_conv_w': dense(k[25], (DEPTH, FFN_CONV, 2 * FFN_HIDDEN), FFN_CONV),
        'ffn_conv_b': small(k[26], (DEPTH, 2 * FFN_HIDDEN)),
        'ffn_w_down': dense(k[27], (DEPTH, FFN_HIDDEN, D), FFN_HIDDEN),
    }


def reference(x, positions, norm_mix, norm_ffn, norm_final,
              mla_wq_a, mla_q_norm, mla_wq_b, mla_wkv_a, mla_kv_norm, mla_wkv_b, mla_wo,
              sb_wqkv, sb_wo,
              ssd_w_in, ssd_conv_w, ssd_conv_b, ssd_dt_bias, ssd_a_log, ssd_d, ssd_norm, ssd_w_out,
              ret_w_in, ret_wo,
              ffn_w_up, ffn_conv_w, ffn_conv_b, ffn_w_down):
    for i in range(DEPTH):
        m, j = i % N_MIXERS, i // N_MIXERS
        h = _rmsnorm(x, norm_mix[i])
        if m == 0:
            mix = _mla(h, positions, mla_wq_a[j], mla_q_norm[j], mla_wq_b[j],
                       mla_wkv_a[j], mla_kv_norm[j], mla_wkv_b[j], mla_wo[j])
        elif m == 1:
            mix = _stick_breaking(h, sb_wqkv[j], sb_wo[j])
        elif m == 2:
            mix = _mamba2(h, ssd_w_in[j], ssd_conv_w[j], ssd_conv_b[j], ssd_dt_bias[j],
                          ssd_a_log[j], ssd_d[j], ssd_norm[j], ssd_w_out[j])
        else:
            mix = _retention(h, positions, ret_w_in[j], ret_wo[j])
        x = x + mix
        x = x + _conv_ffn(_rmsnorm(x, norm_ffn[i]), ffn_w_up[i], ffn_conv_w[i], ffn_conv_b[i], ffn_w_down[i])
    return _rmsnorm(x, norm_final)
```

```python
import functools

import jax
import jax.numpy as jnp
from jax import lax
from jax.experimental import pallas as pl
from jax.experimental.pallas import tpu as pltpu

F32 = jnp.float32
BF16 = jnp.bfloat16

RMS_EPS = 1e-6
ROPE_BASE = 10000.0

MLA_HEADS = 16
MLA_Q_RANK = 512
MLA_KV_RANK = 512
MLA_NOPE = 128
MLA_ROPE = 64
MLA_V = 128
SB_HEADS = 16
SB_HEAD_DIM = 128
SSD_HEAD_DIM = 64
SSD_GROUPS = 8
SSD_STATE = 128
SSD_CONV = 4
RET_HEADS = 8
FFN_CONV = 3

LANES = 128
SUBLANES = 8
VMEM_LIMIT_BYTES = 56 * 1024 * 1024


def _cparams(*sem):
    return pltpu.CompilerParams(dimension_semantics=sem, vmem_limit_bytes=VMEM_LIMIT_BYTES)


def _dot(a, b):
    return jnp.dot(a, b, preferred_element_type=F32)


def _dot_nt(a, b):
    return lax.dot_general(a, b, (((1,), (1,)), ((), ())), preferred_element_type=F32)


def _dot_split(a, b):
    hi = a.astype(BF16)
    lo = (a - hi.astype(F32)).astype(BF16)
    return _dot(hi, b) + _dot(lo, b)


def _rms_rows(xf, w):
    ms = jnp.mean(xf * xf, axis=-1, keepdims=True)
    return xf * lax.rsqrt(ms + RMS_EPS) * w


def _silu(x):
    return x * jax.nn.sigmoid(x)


def _softplus(x):
    return jnp.maximum(x, 0.0) + jnp.log(1.0 + jnp.exp(-jnp.abs(x)))


def _rope_tables_body(pos_ref, inv_a_ref, sgn_a_ref, inv_r_ref, cos_a_ref, sin_a_ref, cos_r_ref, sin_r_ref):
    pos = pos_ref[...]
    ang_a = pos * inv_a_ref[...]
    cos_a_ref[...] = jnp.cos(ang_a)
    sin_a_ref[...] = jnp.sin(ang_a) * sgn_a_ref[...]
    ang_r = pos * inv_r_ref[...]
    cos_r_ref[...] = jnp.cos(ang_r)
    sin_r_ref[...] = jnp.sin(ang_r)


def _rope_tables(positions, ret_dk):
    T = positions.size
    tm = 1024
    pos = positions.reshape(T, 1).astype(F32)
    half_a = MLA_ROPE // 2
    inv_a = 1.0 / (ROPE_BASE ** (jnp.arange(half_a, dtype=F32) * (2.0 / MLA_ROPE)))
    z = jnp.zeros((half_a,), F32)
    inv_a = jnp.concatenate([inv_a, z, inv_a, z]).reshape(1, LANES)
    o = jnp.ones((2 * half_a,), F32)
    sgn_a = jnp.concatenate([-o, o]).reshape(1, LANES)
    half_r = ret_dk // 2
    assert half_r == LANES
    inv_r = (1.0 / (ROPE_BASE ** (jnp.arange(half_r, dtype=F32) * (2.0 / ret_dk)))).reshape(1, LANES)
    row = pl.BlockSpec((1, LANES), lambda i: (0, 0))
    tab = pl.BlockSpec((tm, LANES), lambda i: (i, 0))
    shp = jax.ShapeDtypeStruct((T, LANES), F32)
    return pl.pallas_call(
        _rope_tables_body,
        grid=(T // tm,),
        in_specs=[pl.BlockSpec((tm, 1), lambda i: (i, 0)), row, row, row],
        out_specs=[tab, tab, tab, tab],
        out_shape=[shp, shp, shp, shp],
        compiler_params=_cparams("parallel"),
        name="rope_tables",
    )(pos, inv_a, sgn_a, inv_r)


def _proj_body(*refs, has_norm, epi, has_res):
    it = iter(refs)
    a_ref = next(it)
    nw_ref = next(it) if has_norm else None
    w_ref = next(it)
    res_ref = next(it) if has_res else None
    cos_ref = sin_ref = None
    if epi is not None:
        cos_ref = next(it)
        sin_ref = next(it)
    o_ref = next(it)
    abf_ref = next(it) if has_norm else None

    if has_norm:
        @pl.when(pl.program_id(1) == 0)
        def _():
            abf_ref[...] = _rms_rows(a_ref[...].astype(F32), nw_ref[...]).astype(BF16)
        a = abf_ref[...]
    else:
        a = a_ref[...]
    acc = _dot(a, w_ref[...])
    if epi == "rope_half":
        c = cos_ref[...]
        s = sin_ref[...]
        parts = []
        for j in range(acc.shape[1] // (2 * LANES)):
            x1 = acc[:, (2 * j) * LANES:(2 * j + 1) * LANES]
            x2 = acc[:, (2 * j + 1) * LANES:(2 * j + 2) * LANES]
            parts += [x1 * c - x2 * s, x2 * c + x1 * s]
        acc = jnp.concatenate(parts, axis=1)
    elif epi == "rope_group":
        c = cos_ref[...]
        s = sin_ref[...]
        parts = []
        for j in range(acc.shape[1] // (2 * LANES)):
            g = acc[:, (2 * j + 1) * LANES:(2 * j + 2) * LANES]
            parts += [acc[:, (2 * j) * LANES:(2 * j + 1) * LANES], g * c + pltpu.roll(g, LANES // 2, 1) * s]
        acc = jnp.concatenate(parts, axis=1)
    if has_res:
        acc = acc + res_ref[...]
    o_ref[...] = acc.astype(o_ref.dtype)


def _proj(a, w, *, norm_w=None, res=None, epi=None, cos=None, sin=None, out_dtype=F32, tm=512, tn=512, name="proj"):
    T, K = a.shape
    N = w.shape[1]
    tn = min(tn, N)
    assert T % tm == 0 and N % tn == 0
    has_norm = norm_w is not None
    has_res = res is not None
    in_specs = [pl.BlockSpec((tm, K), lambda m, n: (m, 0))]
    args = [a]
    if has_norm:
        in_specs.append(pl.BlockSpec((1, K), lambda m, n: (0, 0)))
        args.append(norm_w.reshape(1, K).astype(F32))
    in_specs.append(pl.BlockSpec((K, tn), lambda m, n: (0, n)))
    args.append(w)
    if has_res:
        in_specs.append(pl.BlockSpec((tm, tn), lambda m, n: (m, n)))
        args.append(res)
    if epi is not None:
        in_specs += [pl.BlockSpec((tm, LANES), lambda m, n: (m, 0))] * 2
        args += [cos, sin]
    scratch = [pltpu.VMEM((tm, K), BF16)] if has_norm else []
    return pl.pallas_call(
        functools.partial(_proj_body, has_norm=has_norm, epi=epi, has_res=has_res),
        grid=(T // tm, N // tn),
        in_specs=in_specs,
        out_specs=pl.BlockSpec((tm, tn), lambda m, n: (m, n)),
        out_shape=jax.ShapeDtypeStruct((T, N), out_dtype),
        scratch_shapes=scratch,
        compiler_params=_cparams("parallel", "arbitrary"),
        name=name,
    )(*args)


def _rmsnorm_body(x_ref, w_ref, o_ref):
    o_ref[...] = _rms_rows(x_ref[...], w_ref[...])


def _rmsnorm(x, w, tm=512):
    T, D = x.shape
    return pl.pallas_call(
        _rmsnorm_body,
        grid=(T // tm,),
        in_specs=[pl.BlockSpec((tm, D), lambda m: (m, 0)), pl.BlockSpec((1, D), lambda m: (0, 0))],
        out_specs=pl.BlockSpec((tm, D), lambda m: (m, 0)),
        out_shape=jax.ShapeDtypeStruct((T, D), F32),
        compiler_params=_cparams("parallel"),
        name="final_norm",
    )(x, w.reshape(1, D).astype(F32))


def _ffn_body(x_ref, nw_ref, wg_ref, wu_ref, cg_ref, cu_ref, wd_ref, o_ref,
              h_ref, bg_ref, bu_ref, carry_g_ref, carry_u_ref, *, tiles_per_seq):
    m = pl.program_id(0)
    f = pl.program_id(1)
    tm = x_ref.shape[0]

    @pl.when(f == 0)
    def _():
        x = x_ref[...]
        h_ref[...] = _rms_rows(x, nw_ref[...]).astype(BF16)
        o_ref[...] = x

    h = h_ref[...]
    seq_start = m % tiles_per_seq == 0

    def conv(u, buf_ref, carry_ref, cw_ref):
        @pl.when(seq_start)
        def _():
            buf_ref[0:SUBLANES, :] = jnp.zeros((SUBLANES, u.shape[1]), F32)

        @pl.when(jnp.logical_not(seq_start))
        def _():
            buf_ref[0:SUBLANES, :] = carry_ref[f]

        buf_ref[SUBLANES:SUBLANES + tm, :] = u
        carry_ref[f] = u[tm - SUBLANES:tm, :]
        cw = cw_ref[...]
        return (cw[0:1, :] * buf_ref[SUBLANES - 2:SUBLANES - 2 + tm, :]
                + cw[1:2, :] * buf_ref[SUBLANES - 1:SUBLANES - 1 + tm, :]
                + cw[2:3, :] * u + cw[3:4, :])

    gate = conv(_dot(h, wg_ref[...]), bg_ref, carry_g_ref, cg_ref)
    up = conv(_dot(h, wu_ref[...]), bu_ref, carry_u_ref, cu_ref)
    act = (_silu(gate) * up).astype(BF16)
    o_ref[...] += _dot(act, wd_ref[...])


def _ffn(x, norm_w, wg, wu, cg, cu, wd, seq_len, tm=512, tf=512):
    T, D = x.shape
    F = wg.shape[1]
    assert T % tm == 0 and F % tf == 0 and seq_len % tm == 0
    nf = F // tf
    return pl.pallas_call(
        functools.partial(_ffn_body, tiles_per_seq=seq_len // tm),
        grid=(T // tm, nf),
        in_specs=[
            pl.BlockSpec((tm, D), lambda m, f: (m, 0)),
            pl.BlockSpec((1, D), lambda m, f: (0, 0)),
            pl.BlockSpec((D, tf), lambda m, f: (0, f)),
            pl.BlockSpec((D, tf), lambda m, f: (0, f)),
            pl.BlockSpec((SUBLANES, tf), lambda m, f: (0, f)),
            pl.BlockSpec((SUBLANES, tf), lambda m, f: (0, f)),
            pl.BlockSpec((tf, D), lambda m, f: (f, 0)),
        ],
        out_specs=pl.BlockSpec((tm, D), lambda m, f: (m, 0)),
        out_shape=jax.ShapeDtypeStruct((T, D), F32),
        scratch_shapes=[
            pltpu.VMEM((tm, D), BF16),
            pltpu.VMEM((SUBLANES + tm, tf), F32),
            pltpu.VMEM((SUBLANES + tm, tf), F32),
            pltpu.VMEM((nf, SUBLANES, tf), F32),
            pltpu.VMEM((nf, SUBLANES, tf), F32),
        ],
        compiler_params=_cparams("arbitrary", "arbitrary"),
        name="conv_ffn",
    )(x, norm_w.reshape(1, D).astype(F32), wg, wu, cg, cu, wd)


def _conv_rows(w, b):
    K, C = w.shape
    return jnp.concatenate([w, b.reshape(1, C), jnp.zeros((SUBLANES - K - 1, C), F32)], axis=0).astype(F32)


def _mla_down_body(x_ref, nw_ref, w_ref, qn_ref, kvn_ref, cos_ref, sin_ref, cq_ref, ckv_ref, kpe_ref):
    h = _rms_rows(x_ref[...], nw_ref[...]).astype(BF16)
    r = _dot(h, w_ref[...])
    cq_ref[...] = _rms_rows(r[:, :MLA_Q_RANK], qn_ref[...]).astype(BF16)
    ckv_ref[...] = _rms_rows(r[:, MLA_Q_RANK:MLA_Q_RANK + MLA_KV_RANK], kvn_ref[...]).astype(BF16)
    g = r[:, MLA_Q_RANK + MLA_KV_RANK:]
    kpe_ref[...] = (g * cos_ref[...] + pltpu.roll(g, LANES // 2, 1) * sin_ref[...]).astype(BF16)


def _mla_down(x, norm_w, w, q_norm, kv_norm, cos, sin, tm=512):
    T, D = x.shape
    N = w.shape[1]
    row = lambda n: pl.BlockSpec((1, n), lambda m: (0, 0))
    tile = lambda n: pl.BlockSpec((tm, n), lambda m: (m, 0))
    return pl.pallas_call(
        _mla_down_body,
        grid=(T // tm,),
        in_specs=[tile(D), row(D), pl.BlockSpec((D, N), lambda m: (0, 0)), row(MLA_Q_RANK), row(MLA_KV_RANK),
                  tile(LANES), tile(LANES)],
        out_specs=[tile(MLA_Q_RANK), tile(MLA_KV_RANK), tile(LANES)],
        out_shape=[jax.ShapeDtypeStruct((T, MLA_Q_RANK), BF16), jax.ShapeDtypeStruct((T, MLA_KV_RANK), BF16),
                   jax.ShapeDtypeStruct((T, LANES), BF16)],
        compiler_params=_cparams("parallel"),
        name="mla_down",
    )(x, norm_w.reshape(1, D).astype(F32), w, q_norm.reshape(1, -1).astype(F32),
      kv_norm.reshape(1, -1).astype(F32), cos, sin)


def _mla_attn_body(q_ref, kn_ref, kpe_ref, v_ref, o_ref, *, tk):
    i = pl.program_id(2)
    tq = q_ref.shape[0]
    q = q_ref[...]
    qn = q[:, :LANES]
    qr = q[:, LANES:]

    def step(j, carry, masked):
        m, l, acc = carry
        ks = pl.ds(pl.multiple_of(j * tk, tk), tk)
        s = _dot_nt(qn, kn_ref[ks, :]) + _dot_nt(qr, kpe_ref[ks, :])
        if masked:
            row = lax.broadcasted_iota(jnp.int32, (tq, tk), 0)
            col = lax.broadcasted_iota(jnp.int32, (tq, tk), 1)
            s = jnp.where(col <= row, s, -jnp.inf)
        m_new = jnp.maximum(m, jnp.max(s, axis=-1, keepdims=True))
        alpha = jnp.exp(m - m_new)
        p = jnp.exp(s - m_new)
        l = alpha * l + jnp.sum(p, axis=-1, keepdims=True)
        acc = alpha * acc + _dot(p.astype(BF16), v_ref[ks, :])
        return m_new, l, acc

    init = (jnp.full((tq, 1), -jnp.inf, F32), jnp.zeros((tq, 1), F32), jnp.zeros((tq, v_ref.shape[1]), F32))
    carry = lax.fori_loop(0, i, lambda j, c: step(j, c, False), init)
    _, l, acc = step(i, carry, True)
    o_ref[...] = (acc / l).astype(o_ref.dtype)


def _mla_attn(q, kv, kpe, batch, seq_len, tq=256):
    T = q.shape[0]
    H = MLA_HEADS
    nq = seq_len // tq
    return pl.pallas_call(
        functools.partial(_mla_attn_body, tk=tq),
        grid=(batch, H, nq),
        in_specs=[
            pl.BlockSpec((tq, 2 * LANES), lambda b, h, i: (b * nq + i, h)),
            pl.BlockSpec((seq_len, MLA_NOPE), lambda b, h, i: (b, h)),
            pl.BlockSpec((seq_len, LANES), lambda b, h, i: (b, 0)),
            pl.BlockSpec((seq_len, MLA_V), lambda b, h, i: (b, H + h)),
        ],
        out_specs=pl.BlockSpec((tq, MLA_V), lambda b, h, i: (b * nq + i, h)),
        out_shape=jax.ShapeDtypeStruct((T, H * MLA_V), BF16),
        compiler_params=_cparams("parallel", "parallel", "arbitrary"),
        name="mla_attn",
    )(q, kv, kpe, kv)


def _mla_layer(x, norm_w, tabs, wq_a, q_norm, wq_b, wkv_a, kv_norm, wkv_b, wo, batch, seq_len):
    cos_a, sin_a = tabs[0], tabs[1]
    D = x.shape[1]
    H = MLA_HEADS
    half = MLA_ROPE // 2
    scale = (MLA_NOPE + MLA_ROPE) ** -0.5
    zk = jnp.zeros((D, half), F32)
    w_down = jnp.concatenate(
        [wq_a, wkv_a[:, :MLA_KV_RANK], wkv_a[:, MLA_KV_RANK:MLA_KV_RANK + half], zk,
         wkv_a[:, MLA_KV_RANK + half:], zk], axis=1).astype(BF16)
    wq = wq_b.reshape(MLA_Q_RANK, H, MLA_NOPE + MLA_ROPE)
    zq = jnp.zeros((MLA_Q_RANK, H, half), F32)
    wq = jnp.concatenate([wq[:, :, :MLA_NOPE], wq[:, :, MLA_NOPE:MLA_NOPE + half], zq,
                          wq[:, :, MLA_NOPE + half:], zq], axis=2)
    wq = (wq * scale).reshape(MLA_Q_RANK, H * 2 * LANES).astype(BF16)
    wkv = wkv_b.reshape(MLA_KV_RANK, H, 2, MLA_NOPE).transpose(0, 2, 1, 3).reshape(MLA_KV_RANK, 2 * H * MLA_NOPE)
    wkv = wkv.astype(BF16)

    cq, ckv, kpe = _mla_down(x, norm_w, w_down, q_norm, kv_norm, cos_a, sin_a)
    q = _proj(cq, wq, epi="rope_group", cos=cos_a, sin=sin_a, out_dtype=BF16, name="mla_q")
    kv = _proj(ckv, wkv, out_dtype=BF16, name="mla_kv")
    o = _mla_attn(q, kv, kpe, batch, seq_len)
    return _proj(o, wo.astype(BF16), res=x, name="mla_out")


def _sb_attn_body(q_ref, k_ref, v_ref, o_ref, *, tk):
    i = pl.program_id(2)
    tq = q_ref.shape[0]
    q = q_ref[...]
    row = lax.broadcasted_iota(jnp.int32, (tk, tk), 0)
    col = lax.broadcasted_iota(jnp.int32, (tk, tk), 1)
    after = jnp.where(row > col, 1.0, 0.0).astype(BF16)
    strict = col < row

    def step(j, carry, masked):
        later_sum, acc = carry
        ks = pl.ds(pl.multiple_of(j * tk, tk), tk)
        z = _dot_nt(q, k_ref[ks, :])
        sp = _softplus(z)
        log_keep = -sp
        if masked:
            log_keep = jnp.where(strict, log_keep, 0.0)
        later = _dot_split(log_keep, after) + later_sum
        a = jnp.exp(z - sp + later)
        if masked:
            a = jnp.where(strict, a, 0.0)
        acc = acc + _dot(a.astype(BF16), v_ref[ks, :])
        return later_sum + jnp.sum(log_keep, axis=-1, keepdims=True), acc

    carry = step(i, (jnp.zeros((tq, 1), F32), jnp.zeros((tq, v_ref.shape[1]), F32)), True)
    _, acc = lax.fori_loop(0, i, lambda t, c: step(i - 1 - t, c, False), carry)
    o_ref[...] = acc.astype(o_ref.dtype)


def _sb_attn(qkv, batch, seq_len, tq=256):
    T = qkv.shape[0]
    H, Dh = SB_HEADS, SB_HEAD_DIM
    nq = seq_len // tq
    return pl.pallas_call(
        functools.partial(_sb_attn_body, tk=tq),
        grid=(batch, H, nq),
        in_specs=[
            pl.BlockSpec((tq, Dh), lambda b, h, i: (b * nq + i, h)),
            pl.BlockSpec((seq_len, Dh), lambda b, h, i: (b, H + h)),
            pl.BlockSpec((seq_len, Dh), lambda b, h, i: (b, 2 * H + h)),
        ],
        out_specs=pl.BlockSpec((tq, Dh), lambda b, h, i: (b * nq + i, h)),
        out_shape=jax.ShapeDtypeStruct((T, H * Dh), BF16),
        compiler_params=_cparams("parallel", "parallel", "arbitrary"),
        name="sb_attn",
    )(qkv, qkv, qkv)


def _sb_layer(x, norm_w, wqkv, wo, batch, seq_len):
    n_q = SB_HEADS * SB_HEAD_DIM
    w = jnp.concatenate([wqkv[:, :n_q] * (SB_HEAD_DIM ** -0.5), wqkv[:, n_q:]], axis=1).astype(BF16)
    qkv = _proj(x, w, norm_w=norm_w, out_dtype=BF16, name="sb_qkv")
    o = _sb_attn(qkv, batch, seq_len)
    return _proj(o, wo.astype(BF16), res=x, name="sb_out")


def _ssd_body(zx_ref, dt_ref, cw_ref, dtb_ref, alog_ref, dfull_ref, nw_ref, exp_ref, o_ref,
              state_ref, carry_ref, buf_ref, y_ref, *, d_inner):
    c = pl.program_id(1)
    L = zx_ref.shape[0]
    G, N, P = SSD_GROUPS, SSD_STATE, SSD_HEAD_DIM
    gw = d_inner // G
    conv_ch = d_inner + 2 * G * N

    @pl.when(c == 0)
    def _():
        state_ref[...] = jnp.zeros_like(state_ref)
        carry_ref[...] = jnp.zeros_like(carry_ref)

    u = zx_ref[:, d_inner:d_inner + conv_ch]
    buf_ref[0:SUBLANES, :] = carry_ref[...]
    buf_ref[SUBLANES:SUBLANES + L, :] = u
    carry_ref[...] = u[L - SUBLANES:L, :]
    cw = cw_ref[...]
    conv = cw[SSD_CONV:SSD_CONV + 1, :] + cw[SSD_CONV - 1:SSD_CONV, :] * u
    for k in range(SSD_CONV - 1):
        off = SUBLANES - (SSD_CONV - 1) + k
        conv = conv + cw[k:k + 1, :] * buf_ref[off:off + L, :]
    xbc = _silu(conv)
    xs = xbc[:, :d_inner]
    bm = xbc[:, d_inner:d_inner + G * N]
    cm = xbc[:, d_inner + G * N:]

    dt = _softplus(dt_ref[...] + dtb_ref[...])
    a = dt * (-jnp.exp(alog_ref[...]))
    r_i = lax.broadcasted_iota(jnp.int32, (L, L), 0)
    c_i = lax.broadcasted_iota(jnp.int32, (L, L), 1)
    causal = c_i <= r_i
    acum_t = _dot_split(a.T, jnp.where(c_i >= r_i, 1.0, 0.0).astype(BF16))
    acum = acum_t.T
    expand = exp_ref[...]
    dt_full = _dot_split(dt, expand)
    acum_full = _dot_split(acum, expand)
    last_full = acum_full[L - 1:L, :]
    xdt = xs * dt_full
    xdt_bf = xdt.astype(BF16)
    xdec_bf = (xdt * jnp.exp(last_full - acum_full)).astype(BF16)
    grow = jnp.exp(acum_full)
    chunk_decay = jnp.exp(last_full)
    lane = lax.broadcasted_iota(jnp.int32, (L, LANES), 1)
    first_half = lane < P

    for g in range(G):
        b_g = bm[:, g * N:(g + 1) * N]
        c_g = cm[:, g * N:(g + 1) * N].astype(BF16)
        cb = _dot_nt(c_g, b_g.astype(BF16))
        prev = state_ref[g]
        gs = slice(g * gw, (g + 1) * gw)
        y_off = _dot(c_g, prev.astype(BF16)) * grow[:, gs]
        parts = []
        for pair in range(gw // LANES):
            h0 = g * (gw // P) + 2 * pair
            xp = xdt_bf[:, h0 * P:h0 * P + LANES]
            ms = []
            for h in (h0, h0 + 1):
                seg = acum[:, h:h + 1] - acum_t[h:h + 1, :]
                ms.append((cb * jnp.exp(jnp.where(causal, seg, -jnp.inf))).astype(BF16))
            zero = jnp.zeros_like(xp)
            parts.append(_dot(ms[0], jnp.where(first_half, xp, zero)) + _dot(ms[1], jnp.where(first_half, zero, xp)))
        y_ref[:, gs] = jnp.concatenate(parts, axis=1) + y_off
        state_ref[g] = prev * chunk_decay[:, gs] + _dot(b_g.T.astype(BF16), xdec_bf[:, gs])

    y = (y_ref[...] + dfull_ref[...] * xs) * _silu(zx_ref[:, :d_inner])
    nw = nw_ref[...]
    for g in range(G):
        gs = slice(g * gw, (g + 1) * gw)
        o_ref[:, gs] = _rms_rows(y[:, gs], nw[:, gs]).astype(o_ref.dtype)


def _ssd_core(zx, dt_raw, cw, dt_bias, a_log, d_full, norm_w, expand, batch, seq_len, chunk=128):
    T = zx.shape[0]
    d_inner = d_full.shape[1]
    conv_ch = cw.shape[1]
    nc = seq_len // chunk
    G, N = SSD_GROUPS, SSD_STATE
    row = lambda n: pl.BlockSpec((1, n), lambda b, c: (0, 0))
    return pl.pallas_call(
        functools.partial(_ssd_body, d_inner=d_inner),
        grid=(batch, nc),
        in_specs=[
            pl.BlockSpec((chunk, zx.shape[1]), lambda b, c: (b * nc + c, 0)),
            pl.BlockSpec((chunk, LANES), lambda b, c: (b * nc + c, 0)),
            pl.BlockSpec((SUBLANES, conv_ch), lambda b, c: (0, 0)),
            row(LANES), row(LANES), row(d_inner), row(d_inner),
            pl.BlockSpec((LANES, d_inner), lambda b, c: (0, 0)),
        ],
        out_specs=pl.BlockSpec((chunk, d_inner), lambda b, c: (b * nc + c, 0)),
        out_shape=jax.ShapeDtypeStruct((T, d_inner), BF16),
        scratch_shapes=[
            pltpu.VMEM((G, N, d_inner // G), F32),
            pltpu.VMEM((SUBLANES, conv_ch), F32),
            pltpu.VMEM((SUBLANES + chunk, conv_ch), F32),
            pltpu.VMEM((chunk, d_inner), F32),
        ],
        compiler_params=_cparams("parallel", "arbitrary"),
        name="ssd_core",
    )(zx, dt_raw, cw, dt_bias, a_log, d_full, norm_w, expand)


def _ssd_layer(x, norm_w, w_in, conv_w, conv_b, dt_bias, a_log, d_skip, ssd_norm, w_out, batch, seq_len):
    heads = d_skip.shape[0]
    d_inner = heads * SSD_HEAD_DIM
    n_zx = w_in.shape[1] - heads
    assert heads <= LANES
    pad = lambda v: jnp.concatenate([v.astype(F32), jnp.zeros((LANES - heads,), F32)]).reshape(1, LANES)
    w_dt = jnp.concatenate([w_in[:, n_zx:], jnp.zeros((w_in.shape[0], LANES - heads), F32)], axis=1).astype(BF16)
    zx = _proj(x, w_in[:, :n_zx].astype(BF16), norm_w=norm_w, name="ssd_in")
    dt_raw = _proj(x, w_dt, norm_w=norm_w, name="ssd_dt")
    expand = (jnp.arange(LANES)[:, None] == (jnp.arange(d_inner) // SSD_HEAD_DIM)[None, :]).astype(BF16)
    d_full = jnp.repeat(d_skip.astype(F32), SSD_HEAD_DIM).reshape(1, d_inner)
    y = _ssd_core(zx, dt_raw, _conv_rows(conv_w, conv_b), pad(dt_bias), pad(a_log), d_full,
                  ssd_norm.reshape(1, d_inner).astype(F32), expand, batch, seq_len)
    return _proj(y, w_out.astype(BF16), res=x, name="ssd_out")


def _ret_body(q_ref, k_ref, v_ref, g_ref, o_ref, state_ref):
    h = pl.program_id(1)
    c = pl.program_id(2)
    L = q_ref.shape[0]

    @pl.when(c == 0)
    def _():
        state_ref[...] = jnp.zeros_like(state_ref)

    log_g = jnp.log(1.0 - jnp.exp2(-5.0 - (jnp.zeros((1, 1), F32) + h.astype(F32))))
    r_i = lax.broadcasted_iota(jnp.int32, (L, L), 0)
    c_i = lax.broadcasted_iota(jnp.int32, (L, L), 1)
    diff = (r_i - c_i).astype(F32)
    d_intra = jnp.where(diff >= 0, jnp.exp(jnp.maximum(diff, 0.0) * log_g), 0.0)
    idx = lax.broadcasted_iota(jnp.int32, (L, 1), 0).astype(F32)
    k_dec = jnp.exp((L - 1.0 - idx) * log_g)
    q_dec = jnp.exp((idx + 1.0) * log_g)

    q = q_ref[...]
    k = k_ref[...]
    v = v_ref[...]
    prev = state_ref[...]
    scores = (_dot_nt(q, k) * d_intra).astype(BF16)
    o = _dot(scores, v) + q_dec * _dot(q, prev.astype(BF16))
    kd_t = (k.astype(F32) * k_dec).T.astype(BF16)
    state_ref[...] = jnp.exp(L * log_g) * prev + _dot(kd_t, v)
    ms = jnp.mean(o * o, axis=-1, keepdims=True)
    o_ref[...] = (_silu(g_ref[...]) * (o * lax.rsqrt(ms + RMS_EPS))).astype(o_ref.dtype)


def _ret_core(qk, v, g, batch, seq_len, dk, dv, chunk=128):
    T = qk.shape[0]
    H = RET_HEADS
    nc = seq_len // chunk
    return pl.pallas_call(
        _ret_body,
        grid=(batch, H, nc),
        in_specs=[
            pl.BlockSpec((chunk, dk), lambda b, h, c: (b * nc + c, h)),
            pl.BlockSpec((chunk, dk), lambda b, h, c: (b * nc + c, H + h)),
            pl.BlockSpec((chunk, dv), lambda b, h, c: (b * nc + c, h)),
            pl.BlockSpec((chunk, dv), lambda b, h, c: (b * nc + c, h)),
        ],
        out_specs=pl.BlockSpec((chunk, dv), lambda b, h, c: (b * nc + c, h)),
        out_shape=jax.ShapeDtypeStruct((T, H * dv), BF16),
        scratch_shapes=[pltpu.VMEM((dk, dv), F32)],
        compiler_params=_cparams("parallel", "parallel", "arbitrary"),
        name="ret_core",
    )(qk, qk, v, g)


def _ret_layer(x, norm_w, tabs, w_in, wo, batch, seq_len):
    cos_r, sin_r = tabs[2], tabs[3]
    D = x.shape[1]
    H = RET_HEADS
    dk = D // H
    dv = 2 * D // H
    n_qk = 2 * H * dk
    n_v = H * dv
    w_qk = jnp.concatenate([w_in[:, :H * dk], w_in[:, H * dk:n_qk] * (dk ** -0.5)], axis=1).astype(BF16)
    qk = _proj(x, w_qk, norm_w=norm_w, epi="rope_half", cos=cos_r, sin=sin_r, out_dtype=BF16, name="ret_qk")
    v = _proj(x, w_in[:, n_qk:n_qk + n_v].astype(BF16), norm_w=norm_w, out_dtype=BF16, name="ret_v")
    g = _proj(x, w_in[:, n_qk + n_v:].astype(BF16), norm_w=norm_w, name="ret_g")
    o = _ret_core(qk, v, g, batch, seq_len, dk, dv)
    return _proj(o, wo.astype(BF16), res=x, name="ret_out")


def kernel(x, positions, norm_mix, norm_ffn, norm_final, mla_wq_a, mla_q_norm, mla_wq_b, mla_wkv_a, mla_kv_norm, mla_wkv_b, mla_wo, sb_wqkv, sb_wo, ssd_w_in, ssd_conv_w, ssd_conv_b, ssd_dt_bias, ssd_a_log, ssd_d, ssd_norm, ssd_w_out, ret_w_in, ret_wo, ffn_w_up, ffn_conv_w, ffn_conv_b, ffn_w_down):
    B, S, D = x.shape
    depth = norm_mix.shape[0]
    n_mixers = 4
    ffn_hidden = ffn_w_down.shape[1]
    tabs = _rope_tables(positions, D // RET_HEADS)
    xt = x.reshape(B * S, D)
    for i in range(depth):
        m, j = i % n_mixers, i // n_mixers
        if m == 0:
            xt = _mla_layer(xt, norm_mix[i], tabs, mla_wq_a[j], mla_q_norm[j], mla_wq_b[j], mla_wkv_a[j],
                            mla_kv_norm[j], mla_wkv_b[j], mla_wo[j], B, S)
        elif m == 1:
            xt = _sb_layer(xt, norm_mix[i], sb_wqkv[j], sb_wo[j], B, S)
        elif m == 2:
            xt = _ssd_layer(xt, norm_mix[i], ssd_w_in[j], ssd_conv_w[j], ssd_conv_b[j], ssd_dt_bias[j],
                            ssd_a_log[j], ssd_d[j], ssd_norm[j], ssd_w_out[j], B, S)
        else:
            xt = _ret_layer(xt, norm_mix[i], tabs, ret_w_in[j], ret_wo[j], B, S)
        cw = _conv_rows(ffn_conv_w[i], ffn_conv_b[i])
        F = ffn_hidden
        xt = _ffn(xt, norm_ffn[i], ffn_w_up[i][:, :F].astype(BF16), ffn_w_up[i][:, F:].astype(BF16),
                  cw[:, :F], cw[:, F:], ffn_w_down[i].astype(BF16), S)
    return _rmsnorm(xt, norm_final).reshape(B, S, D)
```

```python
import functools

import jax
import jax.numpy as jnp
from jax import lax
from jax.experimental import pallas as pl
from jax.experimental.pallas import tpu as pltpu

F32 = jnp.float32
BF16 = jnp.bfloat16

RMS_EPS = 1e-6
ROPE_BASE = 10000.0
LOG2_E = 1.4426950408889634

MLA_HEADS = 16
MLA_Q_RANK = 512
MLA_KV_RANK = 512
MLA_NOPE = 128
MLA_ROPE = 64
MLA_V = 128
SB_HEADS = 16
SB_HEAD_DIM = 128
SSD_HEAD_DIM = 64
SSD_GROUPS = 8
SSD_STATE = 128
SSD_CONV = 4
RET_HEADS = 8
FFN_CONV = 3

LANES = 128
SUBLANES = 8
VMEM_LIMIT_BYTES = 56 * 1024 * 1024
PROJ_VMEM_BUDGET_BYTES = 44 * 1024 * 1024


def _cparams(*sem):
    return pltpu.CompilerParams(dimension_semantics=sem, vmem_limit_bytes=VMEM_LIMIT_BYTES)


def _dot(a, b):
    return jnp.dot(a, b, preferred_element_type=F32)


def _dot_nt(a, b):
    return lax.dot_general(a, b, (((1,), (1,)), ((), ())), preferred_element_type=F32)


def _dot_split(a, b):
    hi = a.astype(BF16)
    lo = (a - hi.astype(F32)).astype(BF16)
    return _dot(hi, b) + _dot(lo, b)


def _rms_rows(xf, w):
    ms = jnp.mean(xf * xf, axis=-1, keepdims=True)
    return xf * lax.rsqrt(ms + RMS_EPS) * w


def _silu(x):
    return x * jax.nn.sigmoid(x)


def _neg_abs(x):
    bits = lax.bitcast_convert_type(x, jnp.uint32) | jnp.uint32(0x80000000)
    return lax.bitcast_convert_type(bits, F32)


def _softplus(x):
    return jnp.maximum(x, 0.0) + jnp.log(1.0 + jnp.exp(-jnp.abs(x)))


def _rope_tables_body(pos_ref, inv_a_ref, sgn_a_ref, inv_r_ref, cos_a_ref, sin_a_ref, cos_r_ref, sin_r_ref):
    pos = pos_ref[...]
    ang_a = pos * inv_a_ref[...]
    cos_a_ref[...] = jnp.cos(ang_a)
    sin_a_ref[...] = jnp.sin(ang_a) * sgn_a_ref[...]
    ang_r = pos * inv_r_ref[...]
    cos_r_ref[...] = jnp.cos(ang_r)
    sin_r_ref[...] = jnp.sin(ang_r)


def _rope_tables(positions, ret_dk):
    T = positions.size
    tm = 1024
    pos = positions.reshape(T, 1).astype(F32)
    half_a = MLA_ROPE // 2
    inv_a = 1.0 / (ROPE_BASE ** (jnp.arange(half_a, dtype=F32) * (2.0 / MLA_ROPE)))
    z = jnp.zeros((half_a,), F32)
    inv_a = jnp.concatenate([inv_a, z, inv_a, z]).reshape(1, LANES)
    o = jnp.ones((2 * half_a,), F32)
    sgn_a = jnp.concatenate([-o, o]).reshape(1, LANES)
    half_r = ret_dk // 2
    assert half_r == LANES
    inv_r = (1.0 / (ROPE_BASE ** (jnp.arange(half_r, dtype=F32) * (2.0 / ret_dk)))).reshape(1, LANES)
    row = pl.BlockSpec((1, LANES), lambda i: (0, 0))
    tab = pl.BlockSpec((tm, LANES), lambda i: (i, 0))
    shp = jax.ShapeDtypeStruct((T, LANES), F32)
    return pl.pallas_call(
        _rope_tables_body,
        grid=(T // tm,),
        in_specs=[pl.BlockSpec((tm, 1), lambda i: (i, 0)), row, row, row],
        out_specs=[tab, tab, tab, tab],
        out_shape=[shp, shp, shp, shp],
        compiler_params=_cparams("parallel"),
        name="rope_tables",
    )(pos, inv_a, sgn_a, inv_r)


def _proj_body(*refs, has_norm, epi, has_res):
    it = iter(refs)
    a_ref = next(it)
    nw_ref = next(it) if has_norm else None
    w_ref = next(it)
    res_ref = next(it) if has_res else None
    cos_ref = sin_ref = None
    if epi is not None:
        cos_ref = next(it)
        sin_ref = next(it)
    o_ref = next(it)
    abf_ref = next(it) if has_norm else None

    if has_norm:
        @pl.when(pl.program_id(1) == 0)
        def _():
            abf_ref[...] = _rms_rows(a_ref[...].astype(F32), nw_ref[...]).astype(BF16)
        a = abf_ref[...]
    else:
        a = a_ref[...]
    acc = _dot(a, w_ref[...])
    if epi == "rope_half":
        c = cos_ref[...]
        s = sin_ref[...]
        parts = []
        for j in range(acc.shape[1] // (2 * LANES)):
            x1 = acc[:, (2 * j) * LANES:(2 * j + 1) * LANES]
            x2 = acc[:, (2 * j + 1) * LANES:(2 * j + 2) * LANES]
            parts += [x1 * c - x2 * s, x2 * c + x1 * s]
        acc = jnp.concatenate(parts, axis=1)
    elif epi == "rope_group":
        c = cos_ref[...]
        s = sin_ref[...]
        parts = []
        for j in range(acc.shape[1] // (2 * LANES)):
            g = acc[:, (2 * j + 1) * LANES:(2 * j + 2) * LANES]
            parts += [acc[:, (2 * j) * LANES:(2 * j + 1) * LANES], g * c + pltpu.roll(g, LANES // 2, 1) * s]
        acc = jnp.concatenate(parts, axis=1)
    if has_res:
        acc = acc + res_ref[...]
    o_ref[...] = acc.astype(o_ref.dtype)


def _proj_tiles(T, K, N, a_bytes, out_bytes, has_norm, has_res):
    for tm in (1024, 512, 256):
        if T % tm:
            continue
        for tn in (1024, 512, 256, LANES):
            if N % tn:
                continue
            need = (2 * tm * K * a_bytes + (tm * K * 2 if has_norm else 0) + 2 * K * tn * 2
                    + 2 * tm * tn * out_bytes + (2 * tm * tn * 4 if has_res else 0) + tm * tn * 4)
            if need <= PROJ_VMEM_BUDGET_BYTES:
                return tm, tn
    raise ValueError("no projection tiling fits VMEM")


def _proj(a, w, *, norm_w=None, res=None, epi=None, cos=None, sin=None, out_dtype=F32, name="proj"):
    T, K = a.shape
    N = w.shape[1]
    has_norm = norm_w is not None
    has_res = res is not None
    tm, tn = _proj_tiles(T, K, N, a.dtype.itemsize, jnp.dtype(out_dtype).itemsize, has_norm, has_res)
    in_specs = [pl.BlockSpec((tm, K), lambda m, n: (m, 0))]
    args = [a]
    if has_norm:
        in_specs.append(pl.BlockSpec((1, K), lambda m, n: (0, 0)))
        args.append(norm_w.reshape(1, K).astype(F32))
    in_specs.append(pl.BlockSpec((K, tn), lambda m, n: (0, n)))
    args.append(w)
    if has_res:
        in_specs.append(pl.BlockSpec((tm, tn), lambda m, n: (m, n)))
        args.append(res)
    if epi is not None:
        in_specs += [pl.BlockSpec((tm, LANES), lambda m, n: (m, 0))] * 2
        args += [cos, sin]
    scratch = [pltpu.VMEM((tm, K), BF16)] if has_norm else []
    return pl.pallas_call(
        functools.partial(_proj_body, has_norm=has_norm, epi=epi, has_res=has_res),
        grid=(T // tm, N // tn),
        in_specs=in_specs,
        out_specs=pl.BlockSpec((tm, tn), lambda m, n: (m, n)),
        out_shape=jax.ShapeDtypeStruct((T, N), out_dtype),
        scratch_shapes=scratch,
        compiler_params=_cparams("parallel", "arbitrary"),
        name=name,
    )(*args)


def _rmsnorm_body(x_ref, w_ref, o_ref):
    o_ref[...] = _rms_rows(x_ref[...], w_ref[...])


def _rmsnorm(x, w, tm=512):
    T, D = x.shape
    return pl.pallas_call(
        _rmsnorm_body,
        grid=(T // tm,),
        in_specs=[pl.BlockSpec((tm, D), lambda m: (m, 0)), pl.BlockSpec((1, D), lambda m: (0, 0))],
        out_specs=pl.BlockSpec((tm, D), lambda m: (m, 0)),
        out_shape=jax.ShapeDtypeStruct((T, D), F32),
        compiler_params=_cparams("parallel"),
        name="final_norm",
    )(x, w.reshape(1, D).astype(F32))


def _ffn_body(x_ref, nw_ref, wg_ref, wu_ref, cg_ref, cu_ref, wd_ref, xres_ref, o_ref,
              h_ref, act_ref, bg_ref, bu_ref, carry_g_ref, carry_u_ref, *, tiles_per_seq, nf, sub):
    m = pl.program_id(0)
    f = pl.program_id(1)
    tm = x_ref.shape[0]
    tf = wg_ref.shape[1]

    @pl.when(f == 0)
    def _():
        h_ref[...] = _rms_rows(x_ref[...], nw_ref[...]).astype(BF16)

    @pl.when(jnp.logical_and(f == 0, m == 0))
    def _():
        carry_g_ref[...] = jnp.zeros_like(carry_g_ref)
        carry_u_ref[...] = jnp.zeros_like(carry_u_ref)

    @pl.when(f < nf)
    def _up():
        h = h_ref[...]
        seq_start = m % tiles_per_seq == 0

        def conv(u, cols, buf_ref, carry_ref, cw_ref):
            buf_ref[0:SUBLANES, cols] = jnp.where(seq_start, 0.0, carry_ref[f, :, cols])
            buf_ref[SUBLANES:SUBLANES + tm, cols] = u
            carry_ref[f, :, cols] = u[tm - SUBLANES:tm, :]
            cw = cw_ref[:, cols]
            return (cw[0:1, :] * buf_ref[SUBLANES - 2:SUBLANES - 2 + tm, cols]
                    + cw[1:2, :] * buf_ref[SUBLANES - 1:SUBLANES - 1 + tm, cols]
                    + cw[2:3, :] * u + cw[3:4, :])

        acts = []
        for j in range(tf // sub):
            cols = slice(j * sub, (j + 1) * sub)
            gate = conv(_dot(h, wg_ref[:, cols]), cols, bg_ref, carry_g_ref, cg_ref)
            up = conv(_dot(h, wu_ref[:, cols]), cols, bu_ref, carry_u_ref, cu_ref)
            acts.append((_silu(gate) * up).astype(BF16))
        act_ref[:, pl.ds(pl.multiple_of(f * tf, tf), tf)] = jnp.concatenate(acts, axis=1)

    @pl.when(f >= nf)
    def _down():
        o_ref[...] = xres_ref[...] + _dot(act_ref[...], wd_ref[...])


def _ffn(x, norm_w, wg, wu, cg, cu, wd, seq_len, tm=512, tf=512, td=512, sub=256):
    T, D = x.shape
    F = wg.shape[1]
    assert T % tm == 0 and F % tf == 0 and seq_len % tm == 0 and D % td == 0 and tf % sub == 0
    nf = F // tf
    nd = D // td
    up_blk = lambda m, f: (0, jnp.minimum(f, nf - 1))
    down_blk = lambda m, f: (0, jnp.maximum(f - nf, 0))
    out_blk = lambda m, f: (m, jnp.maximum(f - nf, 0))
    return pl.pallas_call(
        functools.partial(_ffn_body, tiles_per_seq=seq_len // tm, nf=nf, sub=sub),
        grid=(T // tm, nf + nd),
        in_specs=[
            pl.BlockSpec((tm, D), lambda m, f: (m, 0)),
            pl.BlockSpec((1, D), lambda m, f: (0, 0)),
            pl.BlockSpec((D, tf), up_blk),
            pl.BlockSpec((D, tf), up_blk),
            pl.BlockSpec((SUBLANES, tf), up_blk),
            pl.BlockSpec((SUBLANES, tf), up_blk),
            pl.BlockSpec((F, td), down_blk),
            pl.BlockSpec((tm, td), out_blk),
        ],
        out_specs=pl.BlockSpec((tm, td), out_blk),
        out_shape=jax.ShapeDtypeStruct((T, D), F32),
        scratch_shapes=[
            pltpu.VMEM((tm, D), BF16),
            pltpu.VMEM((tm, F), BF16),
            pltpu.VMEM((SUBLANES + tm, tf), F32),
            pltpu.VMEM((SUBLANES + tm, tf), F32),
            pltpu.VMEM((nf, SUBLANES, tf), F32),
            pltpu.VMEM((nf, SUBLANES, tf), F32),
        ],
        compiler_params=_cparams("arbitrary", "arbitrary"),
        name="conv_ffn",
    )(x, norm_w.reshape(1, D).astype(F32), wg, wu, cg, cu, wd, x)


def _conv_rows(w, b):
    K, C = w.shape
    return jnp.concatenate([w, b.reshape(1, C), jnp.zeros((SUBLANES - K - 1, C), F32)], axis=0).astype(F32)


def _mla_down_body(x_ref, nw_ref, w_ref, qn_ref, kvn_ref, cos_ref, sin_ref, cq_ref, ckv_ref, kpe_ref):
    h = _rms_rows(x_ref[...], nw_ref[...]).astype(BF16)
    r = _dot(h, w_ref[...])
    cq_ref[...] = _rms_rows(r[:, :MLA_Q_RANK], qn_ref[...]).astype(BF16)
    ckv_ref[...] = _rms_rows(r[:, MLA_Q_RANK:MLA_Q_RANK + MLA_KV_RANK], kvn_ref[...]).astype(BF16)
    g = r[:, MLA_Q_RANK + MLA_KV_RANK:]
    kpe_ref[...] = (g * cos_ref[...] + pltpu.roll(g, LANES // 2, 1) * sin_ref[...]).astype(BF16)


def _mla_down(x, norm_w, w, q_norm, kv_norm, cos, sin, tm=512):
    T, D = x.shape
    N = w.shape[1]
    row = lambda n: pl.BlockSpec((1, n), lambda m: (0, 0))
    tile = lambda n: pl.BlockSpec((tm, n), lambda m: (m, 0))
    return pl.pallas_call(
        _mla_down_body,
        grid=(T // tm,),
        in_specs=[tile(D), row(D), pl.BlockSpec((D, N), lambda m: (0, 0)), row(MLA_Q_RANK), row(MLA_KV_RANK),
                  tile(LANES), tile(LANES)],
        out_specs=[tile(MLA_Q_RANK), tile(MLA_KV_RANK), tile(LANES)],
        out_shape=[jax.ShapeDtypeStruct((T, MLA_Q_RANK), BF16), jax.ShapeDtypeStruct((T, MLA_KV_RANK), BF16),
                   jax.ShapeDtypeStruct((T, LANES), BF16)],
        compiler_params=_cparams("parallel"),
        name="mla_down",
    )(x, norm_w.reshape(1, D).astype(F32), w, q_norm.reshape(1, -1).astype(F32),
      kv_norm.reshape(1, -1).astype(F32), cos, sin)


def _lane_tile(x, width):
    return jnp.concatenate([x] * (width // LANES), axis=1)


def _mla_attn_body(q_ref, kn_ref, kpe_ref, v_ref, o_ref, kcat_ref, m_ref, l_ref, acc_ref, *, tk, hp):
    i = pl.program_id(2)
    tq = q_ref.shape[0]

    @pl.when(i == 0)
    def _():
        for h in range(hp):
            kcat_ref[h, :, :LANES] = kn_ref[:, h * LANES:(h + 1) * LANES]
            kcat_ref[h, :, LANES:] = kpe_ref[...]

    m_ref[...] = jnp.full(m_ref.shape, -jnp.inf, F32)
    l_ref[...] = jnp.zeros_like(l_ref)
    acc_ref[...] = jnp.zeros_like(acc_ref)
    qs = [q_ref[:, h * 2 * LANES:(h + 1) * 2 * LANES] for h in range(hp)]

    def step(j, masked):
        ks = pl.ds(pl.multiple_of(j * tk, tk), tk)
        ss = [_dot_nt(qs[h], kcat_ref[h, ks, :]) for h in range(hp)]
        for h in range(hp):
            s = ss[h]
            if masked:
                row = lax.broadcasted_iota(jnp.int32, (tq, tk), 0)
                col = lax.broadcasted_iota(jnp.int32, (tq, tk), 1)
                s = jnp.where(col <= row, s, -jnp.inf)
            m_prev = m_ref[h]
            m_new = jnp.maximum(m_prev, jnp.max(s, axis=-1, keepdims=True))
            alpha = jnp.exp2(m_prev - m_new)
            p = jnp.exp2(s - _lane_tile(m_new, tk))
            l_ref[h] = alpha * l_ref[h] + jnp.sum(p, axis=-1, keepdims=True)
            acc_ref[h] = alpha * acc_ref[h] + _dot(p.astype(BF16), v_ref[ks, h * LANES:(h + 1) * LANES])
            m_ref[h] = m_new

    def body(j, c):
        step(j, False)
        return c

    lax.fori_loop(0, i, body, 0)
    step(i, True)
    for h in range(hp):
        o_ref[:, h * LANES:(h + 1) * LANES] = (acc_ref[h] / l_ref[h]).astype(o_ref.dtype)


def _mla_attn(q, kv, kpe, batch, seq_len, tq=512, hp=2):
    T = q.shape[0]
    H = MLA_HEADS
    assert MLA_V == LANES and MLA_NOPE == LANES and H % hp == 0
    nq = seq_len // tq
    ng = H // hp
    return pl.pallas_call(
        functools.partial(_mla_attn_body, tk=tq, hp=hp),
        grid=(batch, ng, nq),
        in_specs=[
            pl.BlockSpec((tq, hp * 2 * LANES), lambda b, g, i: (b * nq + i, g)),
            pl.BlockSpec((seq_len, hp * LANES), lambda b, g, i: (b, g)),
            pl.BlockSpec((seq_len, LANES), lambda b, g, i: (b, 0)),
            pl.BlockSpec((seq_len, hp * LANES), lambda b, g, i: (b, ng + g)),
        ],
        out_specs=pl.BlockSpec((tq, hp * LANES), lambda b, g, i: (b * nq + i, g)),
        out_shape=jax.ShapeDtypeStruct((T, H * MLA_V), BF16),
        scratch_shapes=[
            pltpu.VMEM((hp, seq_len, 2 * LANES), BF16),
            pltpu.VMEM((hp, tq, LANES), F32),
            pltpu.VMEM((hp, tq, LANES), F32),
            pltpu.VMEM((hp, tq, LANES), F32),
        ],
        compiler_params=_cparams("parallel", "parallel", "arbitrary"),
        name="mla_attn",
    )(q, kv, kpe, kv)


def _mla_layer(x, norm_w, tabs, wq_a, q_norm, wq_b, wkv_a, kv_norm, wkv_b, wo, batch, seq_len):
    cos_a, sin_a = tabs[0], tabs[1]
    D = x.shape[1]
    H = MLA_HEADS
    half = MLA_ROPE // 2
    scale = (MLA_NOPE + MLA_ROPE) ** -0.5 * LOG2_E
    zk = jnp.zeros((D, half), F32)
    w_down = jnp.concatenate(
        [wq_a, wkv_a[:, :MLA_KV_RANK], wkv_a[:, MLA_KV_RANK:MLA_KV_RANK + half], zk,
         wkv_a[:, MLA_KV_RANK + half:], zk], axis=1).astype(BF16)
    wq = wq_b.reshape(MLA_Q_RANK, H, MLA_NOPE + MLA_ROPE)
    zq = jnp.zeros((MLA_Q_RANK, H, half), F32)
    wq = jnp.concatenate([wq[:, :, :MLA_NOPE], wq[:, :, MLA_NOPE:MLA_NOPE + half], zq,
                          wq[:, :, MLA_NOPE + half:], zq], axis=2)
    wq = (wq * scale).reshape(MLA_Q_RANK, H * 2 * LANES).astype(BF16)
    wkv = wkv_b.reshape(MLA_KV_RANK, H, 2, MLA_NOPE).transpose(0, 2, 1, 3).reshape(MLA_KV_RANK, 2 * H * MLA_NOPE)
    wkv = wkv.astype(BF16)

    cq, ckv, kpe = _mla_down(x, norm_w, w_down, q_norm, kv_norm, cos_a, sin_a)
    q = _proj(cq, wq, epi="rope_group", cos=cos_a, sin=sin_a, out_dtype=BF16, name="mla_q")
    kv = _proj(ckv, wkv, out_dtype=BF16, name="mla_kv")
    o = _mla_attn(q, kv, kpe, batch, seq_len)
    return _proj(o, wo.astype(BF16), res=x, name="mla_out")


def _sb_attn_body(q_ref, k_ref, v_ref, after_ref, o_ref, acc_ref, drop_ref, *, tk, sub, hp):
    i = pl.program_id(2)
    tq = q_ref.shape[0]
    acc_ref[...] = jnp.zeros_like(acc_ref)
    drop_ref[...] = jnp.zeros_like(drop_ref)
    qs = [q_ref[:, h * LANES:(h + 1) * LANES] for h in range(hp)]
    after = after_ref[...]

    def step(j, masked):
        ks = pl.ds(pl.multiple_of(j * tk, tk), tk)
        zs = [_dot_nt(qs[h], k_ref[ks, h * LANES:(h + 1) * LANES]) for h in range(hp)]
        for h in range(hp):
            z = zs[h]
            sp = jnp.maximum(z, 0.0) + jnp.log2(1.0 + jnp.exp2(_neg_abs(z)))
            if masked:
                row = lax.broadcasted_iota(jnp.int32, (tq, tk), 0)
                col = lax.broadcasted_iota(jnp.int32, (tq, tk), 1)
                strict = col < row
                sp = jnp.where(strict, sp, 0.0)
            drop = drop_ref[h]
            parts = [None] * (tk // sub)
            for b in reversed(range(tk // sub)):
                cs = slice(b * sub, (b + 1) * sub)
                sp_b = sp[:, cs]
                within = _dot(sp_b.astype(BF16), after)
                a_b = jnp.exp2(z[:, cs] - sp_b - within - _lane_tile(drop, sub))
                if masked:
                    a_b = jnp.where(strict[:, cs], a_b, 0.0)
                parts[b] = a_b.astype(BF16)
                drop = drop + jnp.sum(sp_b, axis=-1, keepdims=True)
            acc_ref[h] += _dot(jnp.concatenate(parts, axis=1), v_ref[ks, h * LANES:(h + 1) * LANES])
            drop_ref[h] = drop

    step(i, True)

    def body(t, c):
        step(i - 1 - t, False)
        return c

    lax.fori_loop(0, i, body, 0)
    for h in range(hp):
        o_ref[:, h * LANES:(h + 1) * LANES] = acc_ref[h].astype(o_ref.dtype)


def _sb_attn(qkv, batch, seq_len, tq=512, sub=256, hp=2):
    T = qkv.shape[0]
    H, Dh = SB_HEADS, SB_HEAD_DIM
    assert Dh == LANES and H % hp == 0
    nq = seq_len // tq
    ng = H // hp
    after = (jnp.arange(sub)[:, None] > jnp.arange(sub)[None, :]).astype(BF16)
    return pl.pallas_call(
        functools.partial(_sb_attn_body, tk=tq, sub=sub, hp=hp),
        grid=(batch, ng, nq),
        in_specs=[
            pl.BlockSpec((tq, hp * Dh), lambda b, g, i: (b * nq + i, g)),
            pl.BlockSpec((seq_len, hp * Dh), lambda b, g, i: (b, ng + g)),
            pl.BlockSpec((seq_len, hp * Dh), lambda b, g, i: (b, 2 * ng + g)),
            pl.BlockSpec((sub, sub), lambda b, g, i: (0, 0)),
        ],
        out_specs=pl.BlockSpec((tq, hp * Dh), lambda b, g, i: (b * nq + i, g)),
        out_shape=jax.ShapeDtypeStruct((T, H * Dh), BF16),
        scratch_shapes=[pltpu.VMEM((hp, tq, Dh), F32), pltpu.VMEM((hp, tq, LANES), F32)],
        compiler_params=_cparams("parallel", "parallel", "arbitrary"),
        name="sb_attn",
    )(qkv, qkv, qkv, after)


def _sb_layer(x, norm_w, wqkv, wo, batch, seq_len):
    n_q = SB_HEADS * SB_HEAD_DIM
    q_scale = SB_HEAD_DIM ** -0.5 * LOG2_E
    w = jnp.concatenate([wqkv[:, :n_q] * q_scale, wqkv[:, n_q:]], axis=1).astype(BF16)
    qkv = _proj(x, w, norm_w=norm_w, out_dtype=BF16, name="sb_qkv")
    o = _sb_attn(qkv, batch, seq_len)
    return _proj(o, wo.astype(BF16), res=x, name="sb_out")


def _ssd_body(zx_ref, dt_ref, cw_ref, dtb_ref, alog_ref, dfull_ref, nw_ref, exp_ref, o_ref,
              state_ref, carry_ref, buf_ref, y_ref, *, d_inner):
    c = pl.program_id(1)
    L = zx_ref.shape[0]
    G, N, P = SSD_GROUPS, SSD_STATE, SSD_HEAD_DIM
    gw = d_inner // G
    conv_ch = d_inner + 2 * G * N

    @pl.when(c == 0)
    def _():
        state_ref[...] = jnp.zeros_like(state_ref)
        carry_ref[...] = jnp.zeros_like(carry_ref)

    u = zx_ref[:, d_inner:d_inner + conv_ch]
    buf_ref[0:SUBLANES, :] = carry_ref[...]
    buf_ref[SUBLANES:SUBLANES + L, :] = u
    carry_ref[...] = u[L - SUBLANES:L, :]
    cw = cw_ref[...]
    conv = cw[SSD_CONV:SSD_CONV + 1, :] + cw[SSD_CONV - 1:SSD_CONV, :] * u
    for k in range(SSD_CONV - 1):
        off = SUBLANES - (SSD_CONV - 1) + k
        conv = conv + cw[k:k + 1, :] * buf_ref[off:off + L, :]
    xbc = _silu(conv)
    xs = xbc[:, :d_inner]
    bm = xbc[:, d_inner:d_inner + G * N]
    cm = xbc[:, d_inner + G * N:]

    dt = _softplus(dt_ref[...] + dtb_ref[...])
    a = dt * (-jnp.exp(alog_ref[...]))
    r_i = lax.broadcasted_iota(jnp.int32, (L, L), 0)
    c_i = lax.broadcasted_iota(jnp.int32, (L, L), 1)
    causal = c_i <= r_i
    acum_t = _dot_split(a.T, jnp.where(c_i >= r_i, 1.0, 0.0).astype(BF16))
    acum = acum_t.T
    expand = exp_ref[...]
    dt_full = _dot_split(dt, expand)
    acum_full = _dot_split(acum, expand)
    last_full = acum_full[L - 1:L, :]
    xdt = xs * dt_full
    xdt_bf = xdt.astype(BF16)
    xdec_bf = (xdt * jnp.exp(last_full - acum_full)).astype(BF16)
    grow = jnp.exp(acum_full)
    chunk_decay = jnp.exp(last_full)
    lane = lax.broadcasted_iota(jnp.int32, (L, LANES), 1)
    first_half = lane < P

    for g in range(G):
        b_g = bm[:, g * N:(g + 1) * N]
        c_g = cm[:, g * N:(g + 1) * N].astype(BF16)
        cb = _dot_nt(c_g, b_g.astype(BF16))
        prev = state_ref[g]
        gs = slice(g * gw, (g + 1) * gw)
        y_off = _dot(c_g, prev.astype(BF16)) * grow[:, gs]
        parts = []
        for pair in range(gw // LANES):
            h0 = g * (gw // P) + 2 * pair
            xp = xdt_bf[:, h0 * P:h0 * P + LANES]
            ms = []
            for h in (h0, h0 + 1):
                seg = acum[:, h:h + 1] - acum_t[h:h + 1, :]
                ms.append((cb * jnp.exp(jnp.where(causal, seg, -jnp.inf))).astype(BF16))
            zero = jnp.zeros_like(xp)
            parts.append(_dot(ms[0], jnp.where(first_half, xp, zero)) + _dot(ms[1], jnp.where(first_half, zero, xp)))
        y_ref[:, gs] = jnp.concatenate(parts, axis=1) + y_off
        state_ref[g] = prev * chunk_decay[:, gs] + _dot(b_g.T.astype(BF16), xdec_bf[:, gs])

    y = (y_ref[...] + dfull_ref[...] * xs) * _silu(zx_ref[:, :d_inner])
    nw = nw_ref[...]
    for g in range(G):
        gs = slice(g * gw, (g + 1) * gw)
        o_ref[:, gs] = _rms_rows(y[:, gs], nw[:, gs]).astype(o_ref.dtype)


def _ssd_core(zx, dt_raw, cw, dt_bias, a_log, d_full, norm_w, expand, batch, seq_len, chunk=128):
    T = zx.shape[0]
    d_inner = d_full.shape[1]
    conv_ch = cw.shape[1]
    nc = seq_len // chunk
    G, N = SSD_GROUPS, SSD_STATE
    row = lambda n: pl.BlockSpec((1, n), lambda b, c: (0, 0))
    return pl.pallas_call(
        functools.partial(_ssd_body, d_inner=d_inner),
        grid=(batch, nc),
        in_specs=[
            pl.BlockSpec((chunk, zx.shape[1]), lambda b, c: (b * nc + c, 0)),
            pl.BlockSpec((chunk, LANES), lambda b, c: (b * nc + c, 0)),
            pl.BlockSpec((SUBLANES, conv_ch), lambda b, c: (0, 0)),
            row(LANES), row(LANES), row(d_inner), row(d_inner),
            pl.BlockSpec((LANES, d_inner), lambda b, c: (0, 0)),
        ],
        out_specs=pl.BlockSpec((chunk, d_inner), lambda b, c: (b * nc + c, 0)),
        out_shape=jax.ShapeDtypeStruct((T, d_inner), BF16),
        scratch_shapes=[
            pltpu.VMEM((G, N, d_inner // G), F32),
            pltpu.VMEM((SUBLANES, conv_ch), F32),
            pltpu.VMEM((SUBLANES + chunk, conv_ch), F32),
            pltpu.VMEM((chunk, d_inner), F32),
        ],
        compiler_params=_cparams("parallel", "arbitrary"),
        name="ssd_core",
    )(zx, dt_raw, cw, dt_bias, a_log, d_full, norm_w, expand)


def _ssd_layer(x, norm_w, w_in, conv_w, conv_b, dt_bias, a_log, d_skip, ssd_norm, w_out, batch, seq_len):
    heads = d_skip.shape[0]
    d_inner = heads * SSD_HEAD_DIM
    n_zx = w_in.shape[1] - heads
    assert heads <= LANES
    pad = lambda v: jnp.concatenate([v.astype(F32), jnp.zeros((LANES - heads,), F32)]).reshape(1, LANES)
    w_dt = jnp.concatenate([w_in[:, n_zx:], jnp.zeros((w_in.shape[0], LANES - heads), F32)], axis=1).astype(BF16)
    zx = _proj(x, w_in[:, :n_zx].astype(BF16), norm_w=norm_w, name="ssd_in")
    dt_raw = _proj(x, w_dt, norm_w=norm_w, name="ssd_dt")
    expand = (jnp.arange(LANES)[:, None] == (jnp.arange(d_inner) // SSD_HEAD_DIM)[None, :]).astype(BF16)
    d_full = jnp.repeat(d_skip.astype(F32), SSD_HEAD_DIM).reshape(1, d_inner)
    y = _ssd_core(zx, dt_raw, _conv_rows(conv_w, conv_b), pad(dt_bias), pad(a_log), d_full,
                  ssd_norm.reshape(1, d_inner).astype(F32), expand, batch, seq_len)
    return _proj(y, w_out.astype(BF16), res=x, name="ssd_out")


def _ret_body(q_ref, k_ref, v_ref, g_ref, o_ref, state_ref):
    h = pl.program_id(1)
    c = pl.program_id(2)
    L = q_ref.shape[0]

    @pl.when(c == 0)
    def _():
        state_ref[...] = jnp.zeros_like(state_ref)

    log_g = jnp.log(1.0 - jnp.exp2(-5.0 - (jnp.zeros((1, 1), F32) + h.astype(F32))))
    r_i = lax.broadcasted_iota(jnp.int32, (L, L), 0)
    c_i = lax.broadcasted_iota(jnp.int32, (L, L), 1)
    diff = (r_i - c_i).astype(F32)
    d_intra = jnp.where(diff >= 0, jnp.exp(jnp.maximum(diff, 0.0) * log_g), 0.0)
    idx = lax.broadcasted_iota(jnp.int32, (L, 1), 0).astype(F32)
    k_dec = jnp.exp((L - 1.0 - idx) * log_g)
    q_dec = jnp.exp((idx + 1.0) * log_g)

    q = q_ref[...]
    k = k_ref[...]
    v = v_ref[...]
    prev = state_ref[...]
    scores = (_dot_nt(q, k) * d_intra).astype(BF16)
    o = _dot(scores, v) + q_dec * _dot(q, prev.astype(BF16))
    kd_t = (k.astype(F32) * k_dec).T.astype(BF16)
    state_ref[...] = jnp.exp(L * log_g) * prev + _dot(kd_t, v)
    ms = jnp.mean(o * o, axis=-1, keepdims=True)
    o_ref[...] = (_silu(g_ref[...]) * (o * lax.rsqrt(ms + RMS_EPS))).astype(o_ref.dtype)


def _ret_core(qk, v, g, batch, seq_len, dk, dv, chunk=128):
    T = qk.shape[0]
    H = RET_HEADS
    nc = seq_len // chunk
    return pl.pallas_call(
        _ret_body,
        grid=(batch, H, nc),
        in_specs=[
            pl.BlockSpec((chunk, dk), lambda b, h, c: (b * nc + c, h)),
            pl.BlockSpec((chunk, dk), lambda b, h, c: (b * nc + c, H + h)),
            pl.BlockSpec((chunk, dv), lambda b, h, c: (b * nc + c, h)),
            pl.BlockSpec((chunk, dv), lambda b, h, c: (b * nc + c, h)),
        ],
        out_specs=pl.BlockSpec((chunk, dv), lambda b, h, c: (b * nc + c, h)),
        out_shape=jax.ShapeDtypeStruct((T, H * dv), BF16),
        scratch_shapes=[pltpu.VMEM((dk, dv), F32)],
        compiler_params=_cparams("parallel", "parallel", "arbitrary"),
        name="ret_core",
    )(qk, qk, v, g)


def _ret_layer(x, norm_w, tabs, w_in, wo, batch, seq_len):
    cos_r, sin_r = tabs[2], tabs[3]
    D = x.shape[1]
    H = RET_HEADS
    dk = D // H
    dv = 2 * D // H
    n_qk = 2 * H * dk
    n_v = H * dv
    w_qk = jnp.concatenate([w_in[:, :H * dk], w_in[:, H * dk:n_qk] * (dk ** -0.5)], axis=1).astype(BF16)
    qk = _proj(x, w_qk, norm_w=norm_w, epi="rope_half", cos=cos_r, sin=sin_r, out_dtype=BF16, name="ret_qk")
    v = _proj(x, w_in[:, n_qk:n_qk + n_v].astype(BF16), norm_w=norm_w, out_dtype=BF16, name="ret_v")
    g = _proj(x, w_in[:, n_qk + n_v:].astype(BF16), norm_w=norm_w, name="ret_g")
    o = _ret_core(qk, v, g, batch, seq_len, dk, dv)
    return _proj(o, wo.astype(BF16), res=x, name="ret_out")


def kernel(x, positions, norm_mix, norm_ffn, norm_final, mla_wq_a, mla_q_norm, mla_wq_b, mla_wkv_a, mla_kv_norm, mla_wkv_b, mla_wo, sb_wqkv, sb_wo, ssd_w_in, ssd_conv_w, ssd_conv_b, ssd_dt_bias, ssd_a_log, ssd_d, ssd_norm, ssd_w_out, ret_w_in, ret_wo, ffn_w_up, ffn_conv_w, ffn_conv_b, ffn_w_down):
    B, S, D = x.shape
    depth = norm_mix.shape[0]
    n_mixers = 4
    ffn_hidden = ffn_w_down.shape[1]
    tabs = _rope_tables(positions, D // RET_HEADS)
    xt = x.reshape(B * S, D)
    for i in range(depth):
        m, j = i % n_mixers, i // n_mixers
        if m == 0:
            xt = _mla_layer(xt, norm_mix[i], tabs, mla_wq_a[j], mla_q_norm[j], mla_wq_b[j], mla_wkv_a[j],
                            mla_kv_norm[j], mla_wkv_b[j], mla_wo[j], B, S)
        elif m == 1:
            xt = _sb_layer(xt, norm_mix[i], sb_wqkv[j], sb_wo[j], B, S)
        elif m == 2:
            xt = _ssd_layer(xt, norm_mix[i], ssd_w_in[j], ssd_conv_w[j], ssd_conv_b[j], ssd_dt_bias[j],
                            ssd_a_log[j], ssd_d[j], ssd_norm[j], ssd_w_out[j], B, S)
        else:
            xt = _ret_layer(xt, norm_mix[i], tabs, ret_w_in[j], ret_wo[j], B, S)
        cw = _conv_rows(ffn_conv_w[i], ffn_conv_b[i])
        F = ffn_hidden
        xt = _ffn(xt, norm_ffn[i], ffn_w_up[i][:, :F].astype(BF16), ffn_w_up[i][:, F:].astype(BF16),
                  cw[:, :F], cw[:, F:], ffn_w_down[i].astype(BF16), S)
    return _rmsnorm(xt, norm_final).reshape(B, S, D)
```

```python
import functools
import math

import jax
import jax.numpy as jnp
from jax import lax
from jax.experimental import pallas as pl
from jax.experimental.pallas import tpu as pltpu

F32 = jnp.float32
BF16 = jnp.bfloat16

RMS_EPS = 1e-6
ROPE_BASE = 10000.0
LOG2_E = 1.4426950408889634

MLA_HEADS = 16
MLA_Q_RANK = 512
MLA_KV_RANK = 512
MLA_NOPE = 128
MLA_ROPE = 64
MLA_V = 128
SB_HEADS = 16
SB_HEAD_DIM = 128
SSD_HEAD_DIM = 64
SSD_GROUPS = 8
SSD_STATE = 128
SSD_CONV = 4
RET_HEADS = 8
FFN_CONV = 3

LANES = 128
SUBLANES = 8
VMEM_LIMIT_BYTES = 56 * 1024 * 1024
PROJ_VMEM_BUDGET_BYTES = 44 * 1024 * 1024


def _cparams(*sem):
    return pltpu.CompilerParams(dimension_semantics=sem, vmem_limit_bytes=VMEM_LIMIT_BYTES)


def _dot(a, b):
    return jnp.dot(a, b, preferred_element_type=F32)


def _dot_nt(a, b):
    return lax.dot_general(a, b, (((1,), (1,)), ((), ())), preferred_element_type=F32)


def _dot_split(a, b):
    hi = a.astype(BF16)
    lo = (a - hi.astype(F32)).astype(BF16)
    return _dot(hi, b) + _dot(lo, b)


def _rms_rows(xf, w):
    ms = jnp.mean(xf * xf, axis=-1, keepdims=True)
    return xf * lax.rsqrt(ms + RMS_EPS) * w


def _silu(x):
    return x * jax.nn.sigmoid(x)


def _neg_abs(x):
    bits = lax.bitcast_convert_type(x, jnp.uint32) | jnp.uint32(0x80000000)
    return lax.bitcast_convert_type(bits, F32)


def _softplus(x):
    return jnp.maximum(x, 0.0) + jnp.log(1.0 + jnp.exp(-jnp.abs(x)))


def _rope_tables_body(pos_ref, inv_a_ref, sgn_a_ref, inv_r_ref, cos_a_ref, sin_a_ref, cos_r_ref, sin_r_ref):
    pos = pos_ref[...]
    ang_a = pos * inv_a_ref[...]
    cos_a_ref[...] = jnp.cos(ang_a)
    sin_a_ref[...] = jnp.sin(ang_a) * sgn_a_ref[...]
    ang_r = pos * inv_r_ref[...]
    cos_r_ref[...] = jnp.cos(ang_r)
    sin_r_ref[...] = jnp.sin(ang_r)


def _rope_tables(positions, ret_dk):
    T = positions.size
    tm = 1024
    pos = positions.reshape(T, 1).astype(F32)
    half_a = MLA_ROPE // 2
    inv_a = 1.0 / (ROPE_BASE ** (jnp.arange(half_a, dtype=F32) * (2.0 / MLA_ROPE)))
    z = jnp.zeros((half_a,), F32)
    inv_a = jnp.concatenate([inv_a, z, inv_a, z]).reshape(1, LANES)
    o = jnp.ones((2 * half_a,), F32)
    sgn_a = jnp.concatenate([-o, o]).reshape(1, LANES)
    half_r = ret_dk // 2
    assert half_r == LANES
    inv_r = (1.0 / (ROPE_BASE ** (jnp.arange(half_r, dtype=F32) * (2.0 / ret_dk)))).reshape(1, LANES)
    row = pl.BlockSpec((1, LANES), lambda i: (0, 0))
    tab = pl.BlockSpec((tm, LANES), lambda i: (i, 0))
    shp = jax.ShapeDtypeStruct((T, LANES), F32)
    return pl.pallas_call(
        _rope_tables_body,
        grid=(T // tm,),
        in_specs=[pl.BlockSpec((tm, 1), lambda i: (i, 0)), row, row, row],
        out_specs=[tab, tab, tab, tab],
        out_shape=[shp, shp, shp, shp],
        compiler_params=_cparams("parallel"),
        name="rope_tables",
    )(pos, inv_a, sgn_a, inv_r)


def _proj_body(*refs, has_norm, epi, has_res):
    it = iter(refs)
    a_ref = next(it)
    nw_ref = next(it) if has_norm else None
    w_ref = next(it)
    res_ref = next(it) if has_res else None
    cos_ref = sin_ref = None
    if epi is not None:
        cos_ref = next(it)
        sin_ref = next(it)
    o_ref = next(it)
    abf_ref = next(it) if has_norm else None

    if has_norm:
        @pl.when(pl.program_id(1) == 0)
        def _():
            abf_ref[...] = _rms_rows(a_ref[...].astype(F32), nw_ref[...]).astype(BF16)
        a = abf_ref[...]
    else:
        a = a_ref[...]
    acc = _dot(a, w_ref[...])
    if epi == "rope_half":
        c = cos_ref[...]
        s = sin_ref[...]
        parts = []
        for j in range(acc.shape[1] // (2 * LANES)):
            x1 = acc[:, (2 * j) * LANES:(2 * j + 1) * LANES]
            x2 = acc[:, (2 * j + 1) * LANES:(2 * j + 2) * LANES]
            parts += [x1 * c - x2 * s, x2 * c + x1 * s]
        acc = jnp.concatenate(parts, axis=1)
    elif epi == "rope_group":
        c = cos_ref[...]
        s = sin_ref[...]
        parts = []
        for j in range(acc.shape[1] // (2 * LANES)):
            g = acc[:, (2 * j + 1) * LANES:(2 * j + 2) * LANES]
            parts += [acc[:, (2 * j) * LANES:(2 * j + 1) * LANES], g * c + pltpu.roll(g, LANES // 2, 1) * s]
        acc = jnp.concatenate(parts, axis=1)
    if has_res:
        acc = acc + res_ref[...]
    o_ref[...] = acc.astype(o_ref.dtype)


def _proj_tiles(T, K, N, a_bytes, out_bytes, has_norm, has_res):
    for tm in (1024, 512, 256):
        if T % tm:
            continue
        for tn in (1024, 512, 256, LANES):
            if N % tn:
                continue
            need = (2 * tm * K * a_bytes + (tm * K * 2 if has_norm else 0) + 2 * K * tn * 2
                    + 2 * tm * tn * out_bytes + (2 * tm * tn * 4 if has_res else 0) + tm * tn * 4)
            if need <= PROJ_VMEM_BUDGET_BYTES:
                return tm, tn
    raise ValueError("no projection tiling fits VMEM")


def _proj(a, w, *, cols=None, norm_w=None, res=None, epi=None, cos=None, sin=None, out_dtype=F32, name="proj"):
    T, K = a.shape
    c0, N = (0, w.shape[1]) if cols is None else cols
    has_norm = norm_w is not None
    has_res = res is not None
    tm, tn = _proj_tiles(T, K, N, a.dtype.itemsize, jnp.dtype(out_dtype).itemsize, has_norm, has_res)
    assert c0 % tn == 0
    nb0 = c0 // tn
    in_specs = [pl.BlockSpec((tm, K), lambda m, n: (m, 0))]
    args = [a]
    if has_norm:
        in_specs.append(pl.BlockSpec((1, K), lambda m, n: (0, 0)))
        args.append(norm_w.reshape(1, K).astype(F32))
    in_specs.append(pl.BlockSpec((K, tn), lambda m, n: (0, nb0 + n)))
    args.append(w)
    if has_res:
        in_specs.append(pl.BlockSpec((tm, tn), lambda m, n: (m, n)))
        args.append(res)
    if epi is not None:
        in_specs += [pl.BlockSpec((tm, LANES), lambda m, n: (m, 0))] * 2
        args += [cos, sin]
    scratch = [pltpu.VMEM((tm, K), BF16)] if has_norm else []
    return pl.pallas_call(
        functools.partial(_proj_body, has_norm=has_norm, epi=epi, has_res=has_res),
        grid=(T // tm, N // tn),
        in_specs=in_specs,
        out_specs=pl.BlockSpec((tm, tn), lambda m, n: (m, n)),
        out_shape=jax.ShapeDtypeStruct((T, N), out_dtype),
        scratch_shapes=scratch,
        compiler_params=_cparams("parallel", "arbitrary"),
        name=name,
    )(*args)


def _rmsnorm_body(x_ref, w_ref, o_ref):
    o_ref[...] = _rms_rows(x_ref[...], w_ref[...])


def _rmsnorm(x, w, tm=512):
    T, D = x.shape
    return pl.pallas_call(
        _rmsnorm_body,
        grid=(T // tm,),
        in_specs=[pl.BlockSpec((tm, D), lambda m: (m, 0)), pl.BlockSpec((1, D), lambda m: (0, 0))],
        out_specs=pl.BlockSpec((tm, D), lambda m: (m, 0)),
        out_shape=jax.ShapeDtypeStruct((T, D), F32),
        compiler_params=_cparams("parallel"),
        name="final_norm",
    )(x, w.reshape(1, D).astype(F32))


def _ffn_body(x_ref, nw_ref, wg_ref, wu_ref, cg_ref, cu_ref, wd_ref, xres_ref, o_ref,
              h_ref, act_ref, bg_ref, bu_ref, carry_g_ref, carry_u_ref, *, tiles_per_seq, nf, sub, rows):
    m = pl.program_id(0)
    f = pl.program_id(1)
    tm = x_ref.shape[0]
    tf = wg_ref.shape[1]

    @pl.when(f == 0)
    def _():
        h_ref[...] = _rms_rows(x_ref[...], nw_ref[...]).astype(BF16)

    @pl.when(jnp.logical_and(f == 0, m == 0))
    def _():
        carry_g_ref[...] = jnp.zeros_like(carry_g_ref)
        carry_u_ref[...] = jnp.zeros_like(carry_u_ref)

    @pl.when(f < nf)
    def _up():
        seq_start = m % tiles_per_seq == 0
        bg_ref[0:SUBLANES, :] = jnp.where(seq_start, 0.0, carry_g_ref[f])
        bu_ref[0:SUBLANES, :] = jnp.where(seq_start, 0.0, carry_u_ref[f])

        def conv(u, r0, cols, buf_ref, cw_ref):
            rb = u.shape[0]
            buf_ref[SUBLANES + r0:SUBLANES + r0 + rb, cols] = u
            cw = cw_ref[:, cols]
            return (cw[0:1, :] * buf_ref[SUBLANES + r0 - 2:SUBLANES + r0 - 2 + rb, cols]
                    + cw[1:2, :] * buf_ref[SUBLANES + r0 - 1:SUBLANES + r0 - 1 + rb, cols]
                    + cw[2:3, :] * buf_ref[SUBLANES + r0:SUBLANES + r0 + rb, cols] + cw[3:4, :])

        for j in range(tf // sub):
            cols = slice(j * sub, (j + 1) * sub)
            for r0 in range(0, tm, rows):
                h = h_ref[r0:r0 + rows, :]
                gate = conv(_dot(h, wg_ref[:, cols]), r0, cols, bg_ref, cg_ref)
                up = conv(_dot(h, wu_ref[:, cols]), r0, cols, bu_ref, cu_ref)
                act_ref[r0:r0 + rows, pl.ds(pl.multiple_of(f * tf + j * sub, sub), sub)] = (
                    _silu(gate) * up).astype(BF16)
        carry_g_ref[f] = bg_ref[tm:tm + SUBLANES, :]
        carry_u_ref[f] = bu_ref[tm:tm + SUBLANES, :]

    @pl.when(f >= nf)
    def _down():
        o_ref[...] = xres_ref[...] + _dot(act_ref[...], wd_ref[...])


def _ffn(x, norm_w, w_up, cw, w_down, layer, seq_len, tm=1024, tf=512, td=256, sub=256, rows=256):
    T, D = x.shape
    F = w_down.shape[1]
    assert T % tm == 0 and F % tf == 0 and seq_len % tm == 0 and D % td == 0 and tf % sub == 0 and tm % rows == 0
    nf = F // tf
    nd = D // td
    gate_blk = lambda m, f: (layer, 0, jnp.minimum(f, nf - 1))
    up_blk = lambda m, f: (layer, 0, nf + jnp.minimum(f, nf - 1))
    down_blk = lambda m, f: (layer, 0, jnp.maximum(f - nf, 0))
    out_blk = lambda m, f: (m, jnp.maximum(f - nf, 0))
    return pl.pallas_call(
        functools.partial(_ffn_body, tiles_per_seq=seq_len // tm, nf=nf, sub=sub, rows=rows),
        grid=(T // tm, nf + nd),
        in_specs=[
            pl.BlockSpec((tm, D), lambda m, f: (m, 0)),
            pl.BlockSpec((1, D), lambda m, f: (0, 0)),
            pl.BlockSpec((None, D, tf), gate_blk),
            pl.BlockSpec((None, D, tf), up_blk),
            pl.BlockSpec((None, SUBLANES, tf), gate_blk),
            pl.BlockSpec((None, SUBLANES, tf), up_blk),
            pl.BlockSpec((None, F, td), down_blk),
            pl.BlockSpec((tm, td), out_blk),
        ],
        out_specs=pl.BlockSpec((tm, td), out_blk),
        out_shape=jax.ShapeDtypeStruct((T, D), F32),
        scratch_shapes=[
            pltpu.VMEM((tm, D), BF16),
            pltpu.VMEM((tm, F), BF16),
            pltpu.VMEM((SUBLANES + tm, tf), F32),
            pltpu.VMEM((SUBLANES + tm, tf), F32),
            pltpu.VMEM((nf, SUBLANES, tf), F32),
            pltpu.VMEM((nf, SUBLANES, tf), F32),
        ],
        compiler_params=_cparams("arbitrary", "arbitrary"),
        name="conv_ffn",
    )(x, norm_w.reshape(1, D).astype(F32), w_up, w_up, cw, cw, w_down, x)


def _conv_rows(w, b):
    K, C = w.shape[-2:]
    pad = jnp.zeros(w.shape[:-2] + (SUBLANES - K - 1, C), F32)
    return jnp.concatenate([w.astype(F32), b.astype(F32)[..., None, :], pad], axis=-2)


def _mla_down_body(x_ref, nw_ref, w_ref, qn_ref, kvn_ref, cos_ref, sin_ref, cq_ref, ckv_ref, kpe_ref):
    h = _rms_rows(x_ref[...], nw_ref[...]).astype(BF16)
    r = _dot(h, w_ref[...])
    cq_ref[...] = _rms_rows(r[:, :MLA_Q_RANK], qn_ref[...]).astype(BF16)
    ckv_ref[...] = _rms_rows(r[:, MLA_Q_RANK:MLA_Q_RANK + MLA_KV_RANK], kvn_ref[...]).astype(BF16)
    g = r[:, MLA_Q_RANK + MLA_KV_RANK:]
    kpe_ref[...] = (g * cos_ref[...] + pltpu.roll(g, LANES // 2, 1) * sin_ref[...]).astype(BF16)


def _mla_down(x, norm_w, w, q_norm, kv_norm, cos, sin, tm=512):
    T, D = x.shape
    N = w.shape[1]
    row = lambda n: pl.BlockSpec((1, n), lambda m: (0, 0))
    tile = lambda n: pl.BlockSpec((tm, n), lambda m: (m, 0))
    return pl.pallas_call(
        _mla_down_body,
        grid=(T // tm,),
        in_specs=[tile(D), row(D), pl.BlockSpec((D, N), lambda m: (0, 0)), row(MLA_Q_RANK), row(MLA_KV_RANK),
                  tile(LANES), tile(LANES)],
        out_specs=[tile(MLA_Q_RANK), tile(MLA_KV_RANK), tile(LANES)],
        out_shape=[jax.ShapeDtypeStruct((T, MLA_Q_RANK), BF16), jax.ShapeDtypeStruct((T, MLA_KV_RANK), BF16),
                   jax.ShapeDtypeStruct((T, LANES), BF16)],
        compiler_params=_cparams("parallel"),
        name="mla_down",
    )(x, norm_w.reshape(1, D).astype(F32), w, q_norm.reshape(1, -1).astype(F32),
      kv_norm.reshape(1, -1).astype(F32), cos, sin)


def _lane_tile(x, width):
    return jnp.concatenate([x] * (width // LANES), axis=1)


def _mla_attn_body(q_ref, kn_ref, kpe_ref, v_ref, o_ref, kcat_ref, m_ref, l_ref, acc_ref, *, tk, hp):
    i = pl.program_id(2)
    tq = q_ref.shape[0]

    @pl.when(i == 0)
    def _():
        for h in range(hp):
            kcat_ref[h, :, :LANES] = kn_ref[:, h * LANES:(h + 1) * LANES]
            kcat_ref[h, :, LANES:] = kpe_ref[...]

    m_ref[...] = jnp.full(m_ref.shape, -jnp.inf, F32)
    l_ref[...] = jnp.zeros_like(l_ref)
    acc_ref[...] = jnp.zeros_like(acc_ref)
    qs = [q_ref[:, h * 2 * LANES:(h + 1) * 2 * LANES] for h in range(hp)]

    def step(j, masked):
        ks = pl.ds(pl.multiple_of(j * tk, tk), tk)
        ss = [_dot_nt(qs[h], kcat_ref[h, ks, :]) for h in range(hp)]
        for h in range(hp):
            s = ss[h]
            if masked:
                row = lax.broadcasted_iota(jnp.int32, (tq, tk), 0)
                col = lax.broadcasted_iota(jnp.int32, (tq, tk), 1)
                s = jnp.where(col <= row, s, -jnp.inf)
            m_prev = m_ref[h]
            m_new = jnp.maximum(m_prev, jnp.max(s, axis=-1, keepdims=True))
            alpha = jnp.exp2(m_prev - m_new)
            p = jnp.exp2(s - _lane_tile(m_new, tk))
            l_ref[h] = alpha * l_ref[h] + jnp.sum(p, axis=-1, keepdims=True)
            acc_ref[h] = alpha * acc_ref[h] + _dot(p.astype(BF16), v_ref[ks, h * LANES:(h + 1) * LANES])
            m_ref[h] = m_new

    def body(j, c):
        step(j, False)
        return c

    lax.fori_loop(0, i, body, 0)
    step(i, True)
    for h in range(hp):
        o_ref[:, h * LANES:(h + 1) * LANES] = (acc_ref[h] / l_ref[h]).astype(o_ref.dtype)


def _mla_attn(q, kv, kpe, batch, seq_len, tq=512, hp=2):
    T = q.shape[0]
    H = MLA_HEADS
    assert MLA_V == LANES and MLA_NOPE == LANES and H % hp == 0
    nq = seq_len // tq
    ng = H // hp
    return pl.pallas_call(
        functools.partial(_mla_attn_body, tk=tq, hp=hp),
        grid=(batch, ng, nq),
        in_specs=[
            pl.BlockSpec((tq, hp * 2 * LANES), lambda b, g, i: (b * nq + i, g)),
            pl.BlockSpec((seq_len, hp * LANES), lambda b, g, i: (b, g)),
            pl.BlockSpec((seq_len, LANES), lambda b, g, i: (b, 0)),
            pl.BlockSpec((seq_len, hp * LANES), lambda b, g, i: (b, ng + g)),
        ],
        out_specs=pl.BlockSpec((tq, hp * LANES), lambda b, g, i: (b * nq + i, g)),
        out_shape=jax.ShapeDtypeStruct((T, H * MLA_V), BF16),
        scratch_shapes=[
            pltpu.VMEM((hp, seq_len, 2 * LANES), BF16),
            pltpu.VMEM((hp, tq, LANES), F32),
            pltpu.VMEM((hp, tq, LANES), F32),
            pltpu.VMEM((hp, tq, LANES), F32),
        ],
        compiler_params=_cparams("parallel", "parallel", "arbitrary"),
        name="mla_attn",
    )(q, kv, kpe, kv)


def _mla_layer(x, norm_w, tabs, wq_a, q_norm, wq_b, wkv_a, kv_norm, wkv_b, wo, batch, seq_len):
    cos_a, sin_a = tabs[0], tabs[1]
    D = x.shape[1]
    H = MLA_HEADS
    half = MLA_ROPE // 2
    scale = (MLA_NOPE + MLA_ROPE) ** -0.5 * LOG2_E
    zk = jnp.zeros((D, half), F32)
    w_down = jnp.concatenate(
        [wq_a, wkv_a[:, :MLA_KV_RANK], wkv_a[:, MLA_KV_RANK:MLA_KV_RANK + half], zk,
         wkv_a[:, MLA_KV_RANK + half:], zk], axis=1).astype(BF16)
    wq = wq_b.reshape(MLA_Q_RANK, H, MLA_NOPE + MLA_ROPE)
    zq = jnp.zeros((MLA_Q_RANK, H, half), F32)
    wq = jnp.concatenate([wq[:, :, :MLA_NOPE], wq[:, :, MLA_NOPE:MLA_NOPE + half], zq,
                          wq[:, :, MLA_NOPE + half:], zq], axis=2)
    wq = (wq * scale).reshape(MLA_Q_RANK, H * 2 * LANES).astype(BF16)
    wkv = wkv_b.reshape(MLA_KV_RANK, H, 2, MLA_NOPE).transpose(0, 2, 1, 3).reshape(MLA_KV_RANK, 2 * H * MLA_NOPE)
    wkv = wkv.astype(BF16)

    cq, ckv, kpe = _mla_down(x, norm_w, w_down, q_norm, kv_norm, cos_a, sin_a)
    q = _proj(cq, wq, epi="rope_group", cos=cos_a, sin=sin_a, out_dtype=BF16, name="mla_q")
    kv = _proj(ckv, wkv, out_dtype=BF16, name="mla_kv")
    o = _mla_attn(q, kv, kpe, batch, seq_len)
    return _proj(o, wo.astype(BF16), res=x, name="mla_out")


def _sb_attn_body(q_ref, k_ref, v_ref, after_ref, o_ref, acc_ref, drop_ref, *, tk, sub, hp):
    i = pl.program_id(2)
    tq = q_ref.shape[0]
    acc_ref[...] = jnp.zeros_like(acc_ref)
    drop_ref[...] = jnp.zeros_like(drop_ref)
    qs = [q_ref[:, h * LANES:(h + 1) * LANES] for h in range(hp)]
    after = after_ref[...]

    def step(j, masked):
        ks = pl.ds(pl.multiple_of(j * tk, tk), tk)
        zs = [_dot_nt(qs[h], k_ref[ks, h * LANES:(h + 1) * LANES]) for h in range(hp)]
        for h in range(hp):
            z = zs[h]
            sp = jnp.maximum(z, 0.0) + jnp.log2(1.0 + jnp.exp2(_neg_abs(z)))
            if masked:
                row = lax.broadcasted_iota(jnp.int32, (tq, tk), 0)
                col = lax.broadcasted_iota(jnp.int32, (tq, tk), 1)
                strict = col < row
                sp = jnp.where(strict, sp, 0.0)
            drop = drop_ref[h]
            parts = [None] * (tk // sub)
            for b in reversed(range(tk // sub)):
                cs = slice(b * sub, (b + 1) * sub)
                sp_b = sp[:, cs]
                within = _dot(sp_b.astype(BF16), after)
                a_b = jnp.exp2(z[:, cs] - sp_b - within - _lane_tile(drop, sub))
                if masked:
                    a_b = jnp.where(strict[:, cs], a_b, 0.0)
                parts[b] = a_b.astype(BF16)
                drop = drop + jnp.sum(sp_b, axis=-1, keepdims=True)
            acc_ref[h] += _dot(jnp.concatenate(parts, axis=1), v_ref[ks, h * LANES:(h + 1) * LANES])
            drop_ref[h] = drop

    step(i, True)

    def body(t, c):
        step(i - 1 - t, False)
        return c

    lax.fori_loop(0, i, body, 0)
    for h in range(hp):
        o_ref[:, h * LANES:(h + 1) * LANES] = acc_ref[h].astype(o_ref.dtype)


def _sb_attn(qkv, batch, seq_len, tq=512, sub=256, hp=2):
    T = qkv.shape[0]
    H, Dh = SB_HEADS, SB_HEAD_DIM
    assert Dh == LANES and H % hp == 0
    nq = seq_len // tq
    ng = H // hp
    after = (jnp.arange(sub)[:, None] > jnp.arange(sub)[None, :]).astype(BF16)
    return pl.pallas_call(
        functools.partial(_sb_attn_body, tk=tq, sub=sub, hp=hp),
        grid=(batch, ng, nq),
        in_specs=[
            pl.BlockSpec((tq, hp * Dh), lambda b, g, i: (b * nq + i, g)),
            pl.BlockSpec((seq_len, hp * Dh), lambda b, g, i: (b, ng + g)),
            pl.BlockSpec((seq_len, hp * Dh), lambda b, g, i: (b, 2 * ng + g)),
            pl.BlockSpec((sub, sub), lambda b, g, i: (0, 0)),
        ],
        out_specs=pl.BlockSpec((tq, hp * Dh), lambda b, g, i: (b * nq + i, g)),
        out_shape=jax.ShapeDtypeStruct((T, H * Dh), BF16),
        scratch_shapes=[pltpu.VMEM((hp, tq, Dh), F32), pltpu.VMEM((hp, tq, LANES), F32)],
        compiler_params=_cparams("parallel", "parallel", "arbitrary"),
        name="sb_attn",
    )(qkv, qkv, qkv, after)


def _sb_layer(x, norm_w, wqkv, wo, batch, seq_len):
    n_q = SB_HEADS * SB_HEAD_DIM
    q_scale = SB_HEAD_DIM ** -0.5 * LOG2_E
    col_scale = jnp.where(jnp.arange(wqkv.shape[1]) < n_q, q_scale, 1.0).astype(F32)
    w = (wqkv * col_scale[None, :]).astype(BF16)
    qkv = _proj(x, w, norm_w=norm_w, out_dtype=BF16, name="sb_qkv")
    o = _sb_attn(qkv, batch, seq_len)
    return _proj(o, wo.astype(BF16), res=x, name="sb_out")


def _ssd_body(zx_ref, dt_ref, cw_ref, dtb_ref, alog_ref, dfull_ref, nw_ref, exp_ref, o_ref,
              state_ref, carry_ref, buf_ref, y_ref, *, d_inner):
    c = pl.program_id(1)
    L = zx_ref.shape[0]
    G, N, P = SSD_GROUPS, SSD_STATE, SSD_HEAD_DIM
    gw = d_inner // G
    conv_ch = d_inner + 2 * G * N

    @pl.when(c == 0)
    def _():
        state_ref[...] = jnp.zeros_like(state_ref)
        carry_ref[...] = jnp.zeros_like(carry_ref)

    u = zx_ref[:, d_inner:d_inner + conv_ch]
    buf_ref[0:SUBLANES, :] = carry_ref[...]
    buf_ref[SUBLANES:SUBLANES + L, :] = u
    carry_ref[...] = u[L - SUBLANES:L, :]
    cw = cw_ref[...]
    conv = cw[SSD_CONV:SSD_CONV + 1, :] + cw[SSD_CONV - 1:SSD_CONV, :] * u
    for k in range(SSD_CONV - 1):
        off = SUBLANES - (SSD_CONV - 1) + k
        conv = conv + cw[k:k + 1, :] * buf_ref[off:off + L, :]
    xbc = _silu(conv)
    xs = xbc[:, :d_inner]
    bm = xbc[:, d_inner:d_inner + G * N]
    cm = xbc[:, d_inner + G * N:]

    dt = _softplus(dt_ref[...] + dtb_ref[...])
    a = dt * (-jnp.exp(alog_ref[...]))
    r_i = lax.broadcasted_iota(jnp.int32, (L, L), 0)
    c_i = lax.broadcasted_iota(jnp.int32, (L, L), 1)
    causal = c_i <= r_i
    acum_t = _dot_split(a.T, jnp.where(c_i >= r_i, 1.0, 0.0).astype(BF16))
    acum = acum_t.T
    expand = exp_ref[...]
    dt_full = _dot_split(dt, expand)
    acum_full = _dot_split(acum, expand)
    last_full = acum_full[L - 1:L, :]
    xdt = xs * dt_full
    xdt_bf = xdt.astype(BF16)
    xdec_bf = (xdt * jnp.exp(last_full - acum_full)).astype(BF16)
    grow = jnp.exp(acum_full)
    chunk_decay = jnp.exp(last_full)
    lane = lax.broadcasted_iota(jnp.int32, (L, LANES), 1)
    first_half = lane < P

    for g in range(G):
        b_g = bm[:, g * N:(g + 1) * N]
        c_g = cm[:, g * N:(g + 1) * N].astype(BF16)
        cb = _dot_nt(c_g, b_g.astype(BF16))
        prev = state_ref[g]
        gs = slice(g * gw, (g + 1) * gw)
        y_off = _dot(c_g, prev.astype(BF16)) * grow[:, gs]
        parts = []
        for pair in range(gw // LANES):
            h0 = g * (gw // P) + 2 * pair
            xp = xdt_bf[:, h0 * P:h0 * P + LANES]
            ms = []
            for h in (h0, h0 + 1):
                seg = acum[:, h:h + 1] - acum_t[h:h + 1, :]
                ms.append((cb * jnp.exp(jnp.where(causal, seg, -jnp.inf))).astype(BF16))
            zero = jnp.zeros_like(xp)
            parts.append(_dot(ms[0], jnp.where(first_half, xp, zero)) + _dot(ms[1], jnp.where(first_half, zero, xp)))
        y_ref[:, gs] = jnp.concatenate(parts, axis=1) + y_off
        state_ref[g] = prev * chunk_decay[:, gs] + _dot(b_g.T.astype(BF16), xdec_bf[:, gs])

    y = (y_ref[...] + dfull_ref[...] * xs) * _silu(zx_ref[:, :d_inner])
    nw = nw_ref[...]
    for g in range(G):
        gs = slice(g * gw, (g + 1) * gw)
        o_ref[:, gs] = _rms_rows(y[:, gs], nw[:, gs]).astype(o_ref.dtype)


def _ssd_core(zx, dt_raw, cw, dt_bias, a_log, d_full, norm_w, expand, batch, seq_len, chunk=128):
    T = zx.shape[0]
    d_inner = d_full.shape[1]
    conv_ch = cw.shape[1]
    nc = seq_len // chunk
    G, N = SSD_GROUPS, SSD_STATE
    row = lambda n: pl.BlockSpec((1, n), lambda b, c: (0, 0))
    return pl.pallas_call(
        functools.partial(_ssd_body, d_inner=d_inner),
        grid=(batch, nc),
        in_specs=[
            pl.BlockSpec((chunk, zx.shape[1]), lambda b, c: (b * nc + c, 0)),
            pl.BlockSpec((chunk, LANES), lambda b, c: (b * nc + c, 0)),
            pl.BlockSpec((SUBLANES, conv_ch), lambda b, c: (0, 0)),
            row(LANES), row(LANES), row(d_inner), row(d_inner),
            pl.BlockSpec((LANES, d_inner), lambda b, c: (0, 0)),
        ],
        out_specs=pl.BlockSpec((chunk, d_inner), lambda b, c: (b * nc + c, 0)),
        out_shape=jax.ShapeDtypeStruct((T, d_inner), BF16),
        scratch_shapes=[
            pltpu.VMEM((G, N, d_inner // G), F32),
            pltpu.VMEM((SUBLANES, conv_ch), F32),
            pltpu.VMEM((SUBLANES + chunk, conv_ch), F32),
            pltpu.VMEM((chunk, d_inner), F32),
        ],
        compiler_params=_cparams("parallel", "arbitrary"),
        name="ssd_core",
    )(zx, dt_raw, cw, dt_bias, a_log, d_full, norm_w, expand)


def _ssd_layer(x, norm_w, w_in, conv_w, conv_b, dt_bias, a_log, d_skip, ssd_norm, w_out, batch, seq_len):
    heads = d_skip.shape[0]
    d_inner = heads * SSD_HEAD_DIM
    n_zx = w_in.shape[1] - heads
    assert heads <= LANES
    pad = lambda v: jnp.concatenate([v.astype(F32), jnp.zeros((LANES - heads,), F32)]).reshape(1, LANES)
    w_dt = jnp.concatenate([w_in[:, n_zx:], jnp.zeros((w_in.shape[0], LANES - heads), F32)], axis=1).astype(BF16)
    zx = _proj(x, w_in.astype(BF16), cols=(0, n_zx), norm_w=norm_w, name="ssd_in")
    dt_raw = _proj(x, w_dt, norm_w=norm_w, name="ssd_dt")
    expand = (jnp.arange(LANES)[:, None] == (jnp.arange(d_inner) // SSD_HEAD_DIM)[None, :]).astype(BF16)
    d_full = jnp.repeat(d_skip.astype(F32), SSD_HEAD_DIM).reshape(1, d_inner)
    y = _ssd_core(zx, dt_raw, _conv_rows(conv_w, conv_b), pad(dt_bias), pad(a_log), d_full,
                  ssd_norm.reshape(1, d_inner).astype(F32), expand, batch, seq_len)
    return _proj(y, w_out.astype(BF16), res=x, name="ssd_out")


def _ret_body(qk_ref, v_ref, g_ref, o_ref, state_ref, *, heads, dk, dv):
    c = pl.program_id(1)
    L = qk_ref.shape[0]

    @pl.when(c == 0)
    def _():
        state_ref[...] = jnp.zeros_like(state_ref)

    r_i = lax.broadcasted_iota(jnp.int32, (L, L), 0)
    c_i = lax.broadcasted_iota(jnp.int32, (L, L), 1)
    diff = (r_i - c_i).astype(F32)
    lower = diff >= 0
    idx = lax.broadcasted_iota(jnp.int32, (L, 1), 0).astype(F32)

    for h in range(heads):
        log_g = math.log(1.0 - 2.0 ** (-5.0 - h))
        d_intra = jnp.where(lower, jnp.exp(jnp.maximum(diff, 0.0) * log_g), 0.0)
        k_dec = jnp.exp((L - 1.0 - idx) * log_g)
        q_dec = jnp.exp((idx + 1.0) * log_g)
        q = qk_ref[:, h * dk:(h + 1) * dk]
        k = qk_ref[:, (heads + h) * dk:(heads + h + 1) * dk]
        vs = slice(h * dv, (h + 1) * dv)
        v = v_ref[:, vs]
        prev = state_ref[h]
        scores = (_dot_nt(q, k) * d_intra).astype(BF16)
        o = _dot(scores, v) + q_dec * _dot(q, prev.astype(BF16))
        kd_t = (k.astype(F32) * k_dec).T.astype(BF16)
        state_ref[h] = math.exp(L * log_g) * prev + _dot(kd_t, v)
        ms = jnp.mean(o * o, axis=-1, keepdims=True)
        o_ref[:, vs] = (_silu(g_ref[:, vs]) * (o * lax.rsqrt(ms + RMS_EPS))).astype(o_ref.dtype)


def _ret_core(qk, v, g, batch, seq_len, dk, dv, chunk=256):
    T = qk.shape[0]
    H = RET_HEADS
    nc = seq_len // chunk
    blk = lambda n: pl.BlockSpec((chunk, n), lambda b, c: (b * nc + c, 0))
    return pl.pallas_call(
        functools.partial(_ret_body, heads=H, dk=dk, dv=dv),
        grid=(batch, nc),
        in_specs=[blk(2 * H * dk), blk(H * dv), blk(H * dv)],
        out_specs=blk(H * dv),
        out_shape=jax.ShapeDtypeStruct((T, H * dv), BF16),
        scratch_shapes=[pltpu.VMEM((H, dk, dv), F32)],
        compiler_params=_cparams("parallel", "arbitrary"),
        name="ret_core",
    )(qk, v, g)


def _ret_layer(x, norm_w, tabs, w_in, wo, batch, seq_len):
    cos_r, sin_r = tabs[2], tabs[3]
    D = x.shape[1]
    H = RET_HEADS
    dk = D // H
    dv = 2 * D // H
    n_qk = 2 * H * dk
    n_v = H * dv
    n_all = w_in.shape[1]
    col = jnp.arange(n_all)
    col_scale = jnp.where((col >= H * dk) & (col < n_qk), dk ** -0.5, 1.0).astype(F32)
    w = (w_in * col_scale[None, :]).astype(BF16)
    qk = _proj(x, w, cols=(0, n_qk), norm_w=norm_w, epi="rope_half", cos=cos_r, sin=sin_r, out_dtype=BF16,
               name="ret_qk")
    v = _proj(x, w, cols=(n_qk, n_v), norm_w=norm_w, out_dtype=BF16, name="ret_v")
    g = _proj(x, w, cols=(n_qk + n_v, n_all - n_qk - n_v), norm_w=norm_w, name="ret_g")
    o = _ret_core(qk, v, g, batch, seq_len, dk, dv)
    return _proj(o, wo.astype(BF16), res=x, name="ret_out")


def kernel(x, positions, norm_mix, norm_ffn, norm_final, mla_wq_a, mla_q_norm, mla_wq_b, mla_wkv_a, mla_kv_norm, mla_wkv_b, mla_wo, sb_wqkv, sb_wo, ssd_w_in, ssd_conv_w, ssd_conv_b, ssd_dt_bias, ssd_a_log, ssd_d, ssd_norm, ssd_w_out, ret_w_in, ret_wo, ffn_w_up, ffn_conv_w, ffn_conv_b, ffn_w_down):
    B, S, D = x.shape
    depth = norm_mix.shape[0]
    n_mixers = 4
    tabs = _rope_tables(positions, D // RET_HEADS)
    w_up = ffn_w_up.astype(BF16)
    w_down = ffn_w_down.astype(BF16)
    cw = _conv_rows(ffn_conv_w, ffn_conv_b)
    xt = x.reshape(B * S, D)
    for i in range(depth):
        m, j = i % n_mixers, i // n_mixers
        if m == 0:
            xt = _mla_layer(xt, norm_mix[i], tabs, mla_wq_a[j], mla_q_norm[j], mla_wq_b[j], mla_wkv_a[j],
                            mla_kv_norm[j], mla_wkv_b[j], mla_wo[j], B, S)
        elif m == 1:
            xt = _sb_layer(xt, norm_mix[i], sb_wqkv[j], sb_wo[j], B, S)
        elif m == 2:
            xt = _ssd_layer(xt, norm_mix[i], ssd_w_in[j], ssd_conv_w[j], ssd_conv_b[j], ssd_dt_bias[j],
                            ssd_a_log[j], ssd_d[j], ssd_norm[j], ssd_w_out[j], B, S)
        else:
            xt = _ret_layer(xt, norm_mix[i], tabs, ret_w_in[j], ret_wo[j], B, S)
        xt = _ffn(xt, norm_ffn[i], w_up, cw, w_down, i, S)
    return _rmsnorm(xt, norm_final).reshape(B, S, D)
```

```python
import functools
import math

import jax
import jax.numpy as jnp
from jax import lax
from jax.experimental import pallas as pl
from jax.experimental.pallas import tpu as pltpu

F32 = jnp.float32
BF16 = jnp.bfloat16

RMS_EPS = 1e-6
ROPE_BASE = 10000.0
LOG2_E = 1.4426950408889634
SB_DEAD_LOG2 = 160.0

MLA_HEADS = 16
MLA_Q_RANK = 512
MLA_KV_RANK = 512
MLA_NOPE = 128
MLA_ROPE = 64
MLA_V = 128
SB_HEADS = 16
SB_HEAD_DIM = 128
SSD_HEAD_DIM = 64
SSD_GROUPS = 8
SSD_STATE = 128
SSD_CONV = 4
RET_HEADS = 8
FFN_CONV = 3

LANES = 128
SUBLANES = 8
VMEM_LIMIT_BYTES = 56 * 1024 * 1024
PROJ_VMEM_BUDGET_BYTES = 44 * 1024 * 1024


def _cparams(*sem):
    return pltpu.CompilerParams(dimension_semantics=sem, vmem_limit_bytes=VMEM_LIMIT_BYTES)


def _dot(a, b):
    return jnp.dot(a, b, preferred_element_type=F32)


def _dot_nt(a, b):
    return lax.dot_general(a, b, (((1,), (1,)), ((), ())), preferred_element_type=F32)


def _dot_split(a, b):
    hi = a.astype(BF16)
    lo = (a - hi.astype(F32)).astype(BF16)
    return _dot(hi, b) + _dot(lo, b)


def _rms_rows(xf, w):
    ms = jnp.mean(xf * xf, axis=-1, keepdims=True)
    return xf * lax.rsqrt(ms + RMS_EPS) * w


def _silu(x):
    return x * jax.nn.sigmoid(x)


def _neg_abs(x):
    bits = lax.bitcast_convert_type(x, jnp.uint32) | jnp.uint32(0x80000000)
    return lax.bitcast_convert_type(bits, F32)


def _softplus(x):
    return jnp.maximum(x, 0.0) + jnp.log(1.0 + jnp.exp(-jnp.abs(x)))


def _rope_tables_body(pos_ref, inv_a_ref, sgn_a_ref, inv_r_ref, cos_a_ref, sin_a_ref, cos_r_ref, sin_r_ref):
    pos = pos_ref[...]
    ang_a = pos * inv_a_ref[...]
    cos_a_ref[...] = jnp.cos(ang_a)
    sin_a_ref[...] = jnp.sin(ang_a) * sgn_a_ref[...]
    ang_r = pos * inv_r_ref[...]
    cos_r_ref[...] = jnp.cos(ang_r)
    sin_r_ref[...] = jnp.sin(ang_r)


def _rope_tables(positions, ret_dk):
    T = positions.size
    tm = 1024
    pos = positions.reshape(T, 1).astype(F32)
    half_a = MLA_ROPE // 2
    inv_a = 1.0 / (ROPE_BASE ** (jnp.arange(half_a, dtype=F32) * (2.0 / MLA_ROPE)))
    z = jnp.zeros((half_a,), F32)
    inv_a = jnp.concatenate([inv_a, z, inv_a, z]).reshape(1, LANES)
    o = jnp.ones((2 * half_a,), F32)
    sgn_a = jnp.concatenate([-o, o]).reshape(1, LANES)
    half_r = ret_dk // 2
    assert half_r == LANES
    inv_r = (1.0 / (ROPE_BASE ** (jnp.arange(half_r, dtype=F32) * (2.0 / ret_dk)))).reshape(1, LANES)
    row = pl.BlockSpec((1, LANES), lambda i: (0, 0))
    tab = pl.BlockSpec((tm, LANES), lambda i: (i, 0))
    shp = jax.ShapeDtypeStruct((T, LANES), F32)
    return pl.pallas_call(
        _rope_tables_body,
        grid=(T // tm,),
        in_specs=[pl.BlockSpec((tm, 1), lambda i: (i, 0)), row, row, row],
        out_specs=[tab, tab, tab, tab],
        out_shape=[shp, shp, shp, shp],
        compiler_params=_cparams("parallel"),
        name="rope_tables",
    )(pos, inv_a, sgn_a, inv_r)


def _proj_body(*refs, has_norm, epi, has_res):
    it = iter(refs)
    a_ref = next(it)
    nw_ref = next(it) if has_norm else None
    w_ref = next(it)
    res_ref = next(it) if has_res else None
    cos_ref = sin_ref = None
    if epi is not None:
        cos_ref = next(it)
        sin_ref = next(it)
    o_ref = next(it)
    abf_ref = next(it) if has_norm else None

    if has_norm:
        @pl.when(pl.program_id(1) == 0)
        def _():
            abf_ref[...] = _rms_rows(a_ref[...].astype(F32), nw_ref[...]).astype(BF16)
        a = abf_ref[...]
    else:
        a = a_ref[...]
    acc = _dot(a, w_ref[...])
    if epi == "rope_half":
        c = cos_ref[...]
        s = sin_ref[...]
        parts = []
        for j in range(acc.shape[1] // (2 * LANES)):
            x1 = acc[:, (2 * j) * LANES:(2 * j + 1) * LANES]
            x2 = acc[:, (2 * j + 1) * LANES:(2 * j + 2) * LANES]
            parts += [x1 * c - x2 * s, x2 * c + x1 * s]
        acc = jnp.concatenate(parts, axis=1)
    elif epi == "rope_group":
        c = cos_ref[...]
        s = sin_ref[...]
        parts = []
        for j in range(acc.shape[1] // (2 * LANES)):
            g = acc[:, (2 * j + 1) * LANES:(2 * j + 2) * LANES]
            parts += [acc[:, (2 * j) * LANES:(2 * j + 1) * LANES], g * c + pltpu.roll(g, LANES // 2, 1) * s]
        acc = jnp.concatenate(parts, axis=1)
    if has_res:
        acc = acc + res_ref[...]
    o_ref[...] = acc.astype(o_ref.dtype)


def _proj_tiles(T, K, N, a_bytes, out_bytes, has_norm, has_res):
    for tm in (1024, 512, 256):
        if T % tm:
            continue
        for tn in (1024, 512, 256, LANES):
            if N % tn:
                continue
            need = (2 * tm * K * a_bytes + (tm * K * 2 if has_norm else 0) + 2 * K * tn * 2
                    + 2 * tm * tn * out_bytes + (2 * tm * tn * 4 if has_res else 0) + tm * tn * 4)
            if need <= PROJ_VMEM_BUDGET_BYTES:
                return tm, tn
    raise ValueError("no projection tiling fits VMEM")


def _proj(a, w, *, cols=None, norm_w=None, res=None, epi=None, cos=None, sin=None, out_dtype=F32, name="proj"):
    T, K = a.shape
    c0, N = (0, w.shape[1]) if cols is None else cols
    has_norm = norm_w is not None
    has_res = res is not None
    tm, tn = _proj_tiles(T, K, N, a.dtype.itemsize, jnp.dtype(out_dtype).itemsize, has_norm, has_res)
    assert c0 % tn == 0
    nb0 = c0 // tn
    in_specs = [pl.BlockSpec((tm, K), lambda m, n: (m, 0))]
    args = [a]
    if has_norm:
        in_specs.append(pl.BlockSpec((1, K), lambda m, n: (0, 0)))
        args.append(norm_w.reshape(1, K).astype(F32))
    in_specs.append(pl.BlockSpec((K, tn), lambda m, n: (0, nb0 + n)))
    args.append(w)
    if has_res:
        in_specs.append(pl.BlockSpec((tm, tn), lambda m, n: (m, n)))
        args.append(res)
    if epi is not None:
        in_specs += [pl.BlockSpec((tm, LANES), lambda m, n: (m, 0))] * 2
        args += [cos, sin]
    scratch = [pltpu.VMEM((tm, K), BF16)] if has_norm else []
    return pl.pallas_call(
        functools.partial(_proj_body, has_norm=has_norm, epi=epi, has_res=has_res),
        grid=(T // tm, N // tn),
        in_specs=in_specs,
        out_specs=pl.BlockSpec((tm, tn), lambda m, n: (m, n)),
        out_shape=jax.ShapeDtypeStruct((T, N), out_dtype),
        scratch_shapes=scratch,
        compiler_params=_cparams("parallel", "arbitrary"),
        name=name,
    )(*args)


def _rmsnorm_body(x_ref, w_ref, o_ref):
    o_ref[...] = _rms_rows(x_ref[...], w_ref[...])


def _rmsnorm(x, w, tm=512):
    T, D = x.shape
    return pl.pallas_call(
        _rmsnorm_body,
        grid=(T // tm,),
        in_specs=[pl.BlockSpec((tm, D), lambda m: (m, 0)), pl.BlockSpec((1, D), lambda m: (0, 0))],
        out_specs=pl.BlockSpec((tm, D), lambda m: (m, 0)),
        out_shape=jax.ShapeDtypeStruct((T, D), F32),
        compiler_params=_cparams("parallel"),
        name="final_norm",
    )(x, w.reshape(1, D).astype(F32))


def _ffn_body(x_ref, nw_ref, wg_ref, wu_ref, cg_ref, cu_ref, wd_ref, xres_ref, o_ref,
              h_ref, act_ref, bg_ref, bu_ref, carry_g_ref, carry_u_ref, *, tiles_per_seq, nf, sub, rows):
    m = pl.program_id(0)
    f = pl.program_id(1)
    tm = x_ref.shape[0]
    tf = wg_ref.shape[1]

    @pl.when(f == 0)
    def _():
        h_ref[...] = _rms_rows(x_ref[...], nw_ref[...]).astype(BF16)

    @pl.when(jnp.logical_and(f == 0, m == 0))
    def _():
        carry_g_ref[...] = jnp.zeros_like(carry_g_ref)
        carry_u_ref[...] = jnp.zeros_like(carry_u_ref)

    @pl.when(f < nf)
    def _up():
        seq_start = m % tiles_per_seq == 0
        bg_ref[0:SUBLANES, :] = jnp.where(seq_start, 0.0, carry_g_ref[f])
        bu_ref[0:SUBLANES, :] = jnp.where(seq_start, 0.0, carry_u_ref[f])

        def conv(u, r0, cols, buf_ref, cw_ref):
            rb = u.shape[0]
            buf_ref[SUBLANES + r0:SUBLANES + r0 + rb, cols] = u
            cw = cw_ref[:, cols]
            return (cw[0:1, :] * buf_ref[SUBLANES + r0 - 2:SUBLANES + r0 - 2 + rb, cols]
                    + cw[1:2, :] * buf_ref[SUBLANES + r0 - 1:SUBLANES + r0 - 1 + rb, cols]
                    + cw[2:3, :] * buf_ref[SUBLANES + r0:SUBLANES + r0 + rb, cols] + cw[3:4, :])

        for j in range(tf // sub):
            cols = slice(j * sub, (j + 1) * sub)
            for r0 in range(0, tm, rows):
                h = h_ref[r0:r0 + rows, :]
                gate = conv(_dot(h, wg_ref[:, cols]), r0, cols, bg_ref, cg_ref)
                up = conv(_dot(h, wu_ref[:, cols]), r0, cols, bu_ref, cu_ref)
                act_ref[r0:r0 + rows, pl.ds(pl.multiple_of(f * tf + j * sub, sub), sub)] = (
                    _silu(gate) * up).astype(BF16)
        carry_g_ref[f] = bg_ref[tm:tm + SUBLANES, :]
        carry_u_ref[f] = bu_ref[tm:tm + SUBLANES, :]

    @pl.when(f >= nf)
    def _down():
        o_ref[...] = xres_ref[...] + _dot(act_ref[...], wd_ref[...])


def _ffn(x, norm_w, w_up, cw, w_down, layer, seq_len, tm=1024, tf=512, td=256, sub=512, rows=512):
    T, D = x.shape
    F = w_down.shape[1]
    assert T % tm == 0 and F % tf == 0 and seq_len % tm == 0 and D % td == 0 and tf % sub == 0 and tm % rows == 0
    nf = F // tf
    nd = D // td
    gate_blk = lambda m, f: (layer, 0, jnp.minimum(f, nf - 1))
    up_blk = lambda m, f: (layer, 0, nf + jnp.minimum(f, nf - 1))
    down_blk = lambda m, f: (layer, 0, jnp.maximum(f - nf, 0))
    out_blk = lambda m, f: (m, jnp.maximum(f - nf, 0))
    return pl.pallas_call(
        functools.partial(_ffn_body, tiles_per_seq=seq_len // tm, nf=nf, sub=sub, rows=rows),
        grid=(T // tm, nf + nd),
        in_specs=[
            pl.BlockSpec((tm, D), lambda m, f: (m, 0)),
            pl.BlockSpec((1, D), lambda m, f: (0, 0)),
            pl.BlockSpec((None, D, tf), gate_blk),
            pl.BlockSpec((None, D, tf), up_blk),
            pl.BlockSpec((None, SUBLANES, tf), gate_blk),
            pl.BlockSpec((None, SUBLANES, tf), up_blk),
            pl.BlockSpec((None, F, td), down_blk),
            pl.BlockSpec((tm, td), out_blk),
        ],
        out_specs=pl.BlockSpec((tm, td), out_blk),
        out_shape=jax.ShapeDtypeStruct((T, D), F32),
        scratch_shapes=[
            pltpu.VMEM((tm, D), BF16),
            pltpu.VMEM((tm, F), BF16),
            pltpu.VMEM((SUBLANES + tm, tf), F32),
            pltpu.VMEM((SUBLANES + tm, tf), F32),
            pltpu.VMEM((nf, SUBLANES, tf), F32),
            pltpu.VMEM((nf, SUBLANES, tf), F32),
        ],
        compiler_params=_cparams("arbitrary", "arbitrary"),
        name="conv_ffn",
    )(x, norm_w.reshape(1, D).astype(F32), w_up, w_up, cw, cw, w_down, x)


def _conv_rows(w, b):
    K, C = w.shape[-2:]
    pad = jnp.zeros(w.shape[:-2] + (SUBLANES - K - 1, C), F32)
    return jnp.concatenate([w.astype(F32), b.astype(F32)[..., None, :], pad], axis=-2)


def _mla_down_body(x_ref, nw_ref, w_ref, qn_ref, kvn_ref, cos_ref, sin_ref, cq_ref, ckv_ref, kpe_ref):
    h = _rms_rows(x_ref[...], nw_ref[...]).astype(BF16)
    r = _dot(h, w_ref[...])
    cq_ref[...] = _rms_rows(r[:, :MLA_Q_RANK], qn_ref[...]).astype(BF16)
    ckv_ref[...] = _rms_rows(r[:, MLA_Q_RANK:MLA_Q_RANK + MLA_KV_RANK], kvn_ref[...]).astype(BF16)
    g = r[:, MLA_Q_RANK + MLA_KV_RANK:]
    kpe_ref[...] = (g * cos_ref[...] + pltpu.roll(g, LANES // 2, 1) * sin_ref[...]).astype(BF16)


def _mla_down(x, norm_w, w, q_norm, kv_norm, cos, sin, tm=512):
    T, D = x.shape
    N = w.shape[1]
    row = lambda n: pl.BlockSpec((1, n), lambda m: (0, 0))
    tile = lambda n: pl.BlockSpec((tm, n), lambda m: (m, 0))
    return pl.pallas_call(
        _mla_down_body,
        grid=(T // tm,),
        in_specs=[tile(D), row(D), pl.BlockSpec((D, N), lambda m: (0, 0)), row(MLA_Q_RANK), row(MLA_KV_RANK),
                  tile(LANES), tile(LANES)],
        out_specs=[tile(MLA_Q_RANK), tile(MLA_KV_RANK), tile(LANES)],
        out_shape=[jax.ShapeDtypeStruct((T, MLA_Q_RANK), BF16), jax.ShapeDtypeStruct((T, MLA_KV_RANK), BF16),
                   jax.ShapeDtypeStruct((T, LANES), BF16)],
        compiler_params=_cparams("parallel"),
        name="mla_down",
    )(x, norm_w.reshape(1, D).astype(F32), w, q_norm.reshape(1, -1).astype(F32),
      kv_norm.reshape(1, -1).astype(F32), cos, sin)


def _lane_tile(x, width):
    return jnp.concatenate([x] * (width // LANES), axis=1)


def _mla_attn_body(q_ref, kn_ref, kpe_ref, v_ref, o_ref, kcat_ref, m_ref, l_ref, acc_ref, *, tk, hp):
    i = pl.program_id(2)
    tq = q_ref.shape[0]

    @pl.when(i == 0)
    def _():
        for h in range(hp):
            kcat_ref[h, :, :LANES] = kn_ref[:, h * LANES:(h + 1) * LANES]
            kcat_ref[h, :, LANES:] = kpe_ref[...]

    m_ref[...] = jnp.full(m_ref.shape, -jnp.inf, F32)
    l_ref[...] = jnp.zeros_like(l_ref)
    acc_ref[...] = jnp.zeros_like(acc_ref)
    qs = [q_ref[:, h * 2 * LANES:(h + 1) * 2 * LANES] for h in range(hp)]

    def step(j, masked):
        ks = pl.ds(pl.multiple_of(j * tk, tk), tk)
        ss = [_dot_nt(qs[h], kcat_ref[h, ks, :]) for h in range(hp)]
        for h in range(hp):
            s = ss[h]
            if masked:
                row = lax.broadcasted_iota(jnp.int32, (tq, tk), 0)
                col = lax.broadcasted_iota(jnp.int32, (tq, tk), 1)
                s = jnp.where(col <= row, s, -jnp.inf)
            m_prev = m_ref[h]
            m_new = jnp.maximum(m_prev, jnp.max(s, axis=-1, keepdims=True))
            alpha = jnp.exp2(m_prev - m_new)
            p = jnp.exp2(s - _lane_tile(m_new, tk))
            l_ref[h] = alpha * l_ref[h] + jnp.sum(p, axis=-1, keepdims=True)
            acc_ref[h] = alpha * acc_ref[h] + _dot(p.astype(BF16), v_ref[ks, h * LANES:(h + 1) * LANES])
            m_ref[h] = m_new

    def body(j, c):
        step(j, False)
        return c

    lax.fori_loop(0, i, body, 0)
    step(i, True)
    for h in range(hp):
        o_ref[:, h * LANES:(h + 1) * LANES] = (acc_ref[h] / l_ref[h]).astype(o_ref.dtype)


def _mla_attn(q, kv, kpe, batch, seq_len, tq=512, hp=2):
    T = q.shape[0]
    H = MLA_HEADS
    assert MLA_V == LANES and MLA_NOPE == LANES and H % hp == 0
    nq = seq_len // tq
    ng = H // hp
    return pl.pallas_call(
        functools.partial(_mla_attn_body, tk=tq, hp=hp),
        grid=(batch, ng, nq),
        in_specs=[
            pl.BlockSpec((tq, hp * 2 * LANES), lambda b, g, i: (b * nq + i, g)),
            pl.BlockSpec((seq_len, hp * LANES), lambda b, g, i: (b, g)),
            pl.BlockSpec((seq_len, LANES), lambda b, g, i: (b, 0)),
            pl.BlockSpec((seq_len, hp * LANES), lambda b, g, i: (b, ng + g)),
        ],
        out_specs=pl.BlockSpec((tq, hp * LANES), lambda b, g, i: (b * nq + i, g)),
        out_shape=jax.ShapeDtypeStruct((T, H * MLA_V), BF16),
        scratch_shapes=[
            pltpu.VMEM((hp, seq_len, 2 * LANES), BF16),
            pltpu.VMEM((hp, tq, LANES), F32),
            pltpu.VMEM((hp, tq, LANES), F32),
            pltpu.VMEM((hp, tq, LANES), F32),
        ],
        compiler_params=_cparams("parallel", "parallel", "arbitrary"),
        name="mla_attn",
    )(q, kv, kpe, kv)


def _mla_layer(x, norm_w, tabs, wq_a, q_norm, wq_b, wkv_a, kv_norm, wkv_b, wo, batch, seq_len):
    cos_a, sin_a = tabs[0], tabs[1]
    D = x.shape[1]
    H = MLA_HEADS
    half = MLA_ROPE // 2
    scale = (MLA_NOPE + MLA_ROPE) ** -0.5 * LOG2_E
    zk = jnp.zeros((D, half), F32)
    w_down = jnp.concatenate(
        [wq_a, wkv_a[:, :MLA_KV_RANK], wkv_a[:, MLA_KV_RANK:MLA_KV_RANK + half], zk,
         wkv_a[:, MLA_KV_RANK + half:], zk], axis=1).astype(BF16)
    wq = wq_b.reshape(MLA_Q_RANK, H, MLA_NOPE + MLA_ROPE)
    zq = jnp.zeros((MLA_Q_RANK, H, half), F32)
    wq = jnp.concatenate([wq[:, :, :MLA_NOPE], wq[:, :, MLA_NOPE:MLA_NOPE + half], zq,
                          wq[:, :, MLA_NOPE + half:], zq], axis=2)
    wq = (wq * scale).reshape(MLA_Q_RANK, H * 2 * LANES).astype(BF16)
    wkv = wkv_b.reshape(MLA_KV_RANK, H, 2, MLA_NOPE).transpose(0, 2, 1, 3).reshape(MLA_KV_RANK, 2 * H * MLA_NOPE)
    wkv = wkv.astype(BF16)

    cq, ckv, kpe = _mla_down(x, norm_w, w_down, q_norm, kv_norm, cos_a, sin_a)
    q = _proj(cq, wq, epi="rope_group", cos=cos_a, sin=sin_a, out_dtype=BF16, name="mla_q")
    kv = _proj(ckv, wkv, out_dtype=BF16, name="mla_kv")
    o = _mla_attn(q, kv, kpe, batch, seq_len)
    return _proj(o, wo.astype(BF16), res=x, name="mla_out")


def _sb_attn_body(q_ref, k_ref, v_ref, after_ref, o_ref, acc_ref, drop_ref, *, tk, sub, hp):
    i = pl.program_id(2)
    tq = q_ref.shape[0]
    acc_ref[...] = jnp.zeros_like(acc_ref)
    drop_ref[...] = jnp.zeros_like(drop_ref)
    qs = [q_ref[:, h * LANES:(h + 1) * LANES] for h in range(hp)]
    after = after_ref[...]

    def step(j, masked):
        ks = pl.ds(pl.multiple_of(j * tk, tk), tk)
        zs = [_dot_nt(qs[h], k_ref[ks, h * LANES:(h + 1) * LANES]) for h in range(hp)]
        for h in range(hp):
            z = zs[h]
            sp = jnp.maximum(z, 0.0) + jnp.log2(1.0 + jnp.exp2(_neg_abs(z)))
            if masked:
                row = lax.broadcasted_iota(jnp.int32, (tq, tk), 0)
                col = lax.broadcasted_iota(jnp.int32, (tq, tk), 1)
                strict = col < row
                sp = jnp.where(strict, sp, 0.0)
            drop = drop_ref[h]
            parts = [None] * (tk // sub)
            for b in reversed(range(tk // sub)):
                cs = slice(b * sub, (b + 1) * sub)
                sp_b = sp[:, cs]
                within = _dot(sp_b.astype(BF16), after)
                a_b = jnp.exp2(z[:, cs] - sp_b - within - _lane_tile(drop, sub))
                if masked:
                    a_b = jnp.where(strict[:, cs], a_b, 0.0)
                parts[b] = a_b.astype(BF16)
                drop = drop + jnp.sum(sp_b, axis=-1, keepdims=True)
            acc_ref[h] += _dot(jnp.concatenate(parts, axis=1), v_ref[ks, h * LANES:(h + 1) * LANES])
            drop_ref[h] = drop

    step(i, True)

    def live():
        return jnp.min(drop_ref[...]) < SB_DEAD_LOG2

    def body(carry):
        t, _ = carry
        step(i - 1 - t, False)
        return t + 1, live()

    lax.while_loop(lambda c: jnp.logical_and(c[0] < i, c[1]), body, (jnp.int32(0), live()))
    for h in range(hp):
        o_ref[:, h * LANES:(h + 1) * LANES] = acc_ref[h].astype(o_ref.dtype)


def _sb_attn(qkv, batch, seq_len, tq=512, sub=256, hp=2):
    T = qkv.shape[0]
    H, Dh = SB_HEADS, SB_HEAD_DIM
    assert Dh == LANES and H % hp == 0
    nq = seq_len // tq
    ng = H // hp
    after = (jnp.arange(sub)[:, None] > jnp.arange(sub)[None, :]).astype(BF16)
    return pl.pallas_call(
        functools.partial(_sb_attn_body, tk=tq, sub=sub, hp=hp),
        grid=(batch, ng, nq),
        in_specs=[
            pl.BlockSpec((tq, hp * Dh), lambda b, g, i: (b * nq + i, g)),
            pl.BlockSpec((seq_len, hp * Dh), lambda b, g, i: (b, ng + g)),
            pl.BlockSpec((seq_len, hp * Dh), lambda b, g, i: (b, 2 * ng + g)),
            pl.BlockSpec((sub, sub), lambda b, g, i: (0, 0)),
        ],
        out_specs=pl.BlockSpec((tq, hp * Dh), lambda b, g, i: (b * nq + i, g)),
        out_shape=jax.ShapeDtypeStruct((T, H * Dh), BF16),
        scratch_shapes=[pltpu.VMEM((hp, tq, Dh), F32), pltpu.VMEM((hp, tq, LANES), F32)],
        compiler_params=_cparams("parallel", "parallel", "arbitrary"),
        name="sb_attn",
    )(qkv, qkv, qkv, after)


def _sb_layer(x, norm_w, wqkv, wo, batch, seq_len):
    n_q = SB_HEADS * SB_HEAD_DIM
    q_scale = SB_HEAD_DIM ** -0.5 * LOG2_E
    col_scale = jnp.where(jnp.arange(wqkv.shape[1]) < n_q, q_scale, 1.0).astype(F32)
    w = (wqkv * col_scale[None, :]).astype(BF16)
    qkv = _proj(x, w, norm_w=norm_w, out_dtype=BF16, name="sb_qkv")
    o = _sb_attn(qkv, batch, seq_len)
    return _proj(o, wo.astype(BF16), res=x, name="sb_out")


def _ssd_body(zx_ref, dt_ref, cw_ref, dtb_ref, alog_ref, dfull_ref, nw_ref, exp_ref, o_ref,
              state_ref, carry_ref, buf_ref, y_ref, *, d_inner):
    c = pl.program_id(1)
    L = zx_ref.shape[0]
    G, N, P = SSD_GROUPS, SSD_STATE, SSD_HEAD_DIM
    gw = d_inner // G
    conv_ch = d_inner + 2 * G * N

    @pl.when(c == 0)
    def _():
        state_ref[...] = jnp.zeros_like(state_ref)
        carry_ref[...] = jnp.zeros_like(carry_ref)

    u = zx_ref[:, d_inner:d_inner + conv_ch]
    buf_ref[0:SUBLANES, :] = carry_ref[...]
    buf_ref[SUBLANES:SUBLANES + L, :] = u
    carry_ref[...] = u[L - SUBLANES:L, :]
    cw = cw_ref[...]
    conv = cw[SSD_CONV:SSD_CONV + 1, :] + cw[SSD_CONV - 1:SSD_CONV, :] * u
    for k in range(SSD_CONV - 1):
        off = SUBLANES - (SSD_CONV - 1) + k
        conv = conv + cw[k:k + 1, :] * buf_ref[off:off + L, :]
    xbc = _silu(conv)
    xs = xbc[:, :d_inner]
    bm = xbc[:, d_inner:d_inner + G * N]
    cm = xbc[:, d_inner + G * N:]

    dt = _softplus(dt_ref[...] + dtb_ref[...])
    a = dt * (-jnp.exp(alog_ref[...]))
    r_i = lax.broadcasted_iota(jnp.int32, (L, L), 0)
    c_i = lax.broadcasted_iota(jnp.int32, (L, L), 1)
    causal = c_i <= r_i
    acum_t = _dot_split(a.T, jnp.where(c_i >= r_i, 1.0, 0.0).astype(BF16))
    acum = acum_t.T
    expand = exp_ref[...]
    dt_full = _dot_split(dt, expand)
    acum_full = _dot_split(acum, expand)
    last_full = acum_full[L - 1:L, :]
    xdt = xs * dt_full
    xdt_bf = xdt.astype(BF16)
    xdec_bf = (xdt * jnp.exp(last_full - acum_full)).astype(BF16)
    grow = jnp.exp(acum_full)
    chunk_decay = jnp.exp(last_full)
    lane = lax.broadcasted_iota(jnp.int32, (L, LANES), 1)
    first_half = lane < P

    for g in range(G):
        b_g = bm[:, g * N:(g + 1) * N]
        c_g = cm[:, g * N:(g + 1) * N].astype(BF16)
        cb = _dot_nt(c_g, b_g.astype(BF16))
        prev = state_ref[g]
        gs = slice(g * gw, (g + 1) * gw)
        y_off = _dot(c_g, prev.astype(BF16)) * grow[:, gs]
        parts = []
        for pair in range(gw // LANES):
            h0 = g * (gw // P) + 2 * pair
            xp = xdt_bf[:, h0 * P:h0 * P + LANES]
            ms = []
            for h in (h0, h0 + 1):
                seg = acum[:, h:h + 1] - acum_t[h:h + 1, :]
                ms.append((cb * jnp.exp(jnp.where(causal, seg, -jnp.inf))).astype(BF16))
            zero = jnp.zeros_like(xp)
            parts.append(_dot(ms[0], jnp.where(first_half, xp, zero)) + _dot(ms[1], jnp.where(first_half, zero, xp)))
        y_ref[:, gs] = jnp.concatenate(parts, axis=1) + y_off
        state_ref[g] = prev * chunk_decay[:, gs] + _dot(b_g.T.astype(BF16), xdec_bf[:, gs])

    y = (y_ref[...] + dfull_ref[...] * xs) * _silu(zx_ref[:, :d_inner])
    nw = nw_ref[...]
    for g in range(G):
        gs = slice(g * gw, (g + 1) * gw)
        o_ref[:, gs] = _rms_rows(y[:, gs], nw[:, gs]).astype(o_ref.dtype)


def _ssd_core(zx, dt_raw, cw, dt_bias, a_log, d_full, norm_w, expand, batch, seq_len, chunk=128):
    T = zx.shape[0]
    d_inner = d_full.shape[1]
    conv_ch = cw.shape[1]
    nc = seq_len // chunk
    G, N = SSD_GROUPS, SSD_STATE
    row = lambda n: pl.BlockSpec((1, n), lambda b, c: (0, 0))
    return pl.pallas_call(
        functools.partial(_ssd_body, d_inner=d_inner),
        grid=(batch, nc),
        in_specs=[
            pl.BlockSpec((chunk, zx.shape[1]), lambda b, c: (b * nc + c, 0)),
            pl.BlockSpec((chunk, LANES), lambda b, c: (b * nc + c, 0)),
            pl.BlockSpec((SUBLANES, conv_ch), lambda b, c: (0, 0)),
            row(LANES), row(LANES), row(d_inner), row(d_inner),
            pl.BlockSpec((LANES, d_inner), lambda b, c: (0, 0)),
        ],
        out_specs=pl.BlockSpec((chunk, d_inner), lambda b, c: (b * nc + c, 0)),
        out_shape=jax.ShapeDtypeStruct((T, d_inner), BF16),
        scratch_shapes=[
            pltpu.VMEM((G, N, d_inner // G), F32),
            pltpu.VMEM((SUBLANES, conv_ch), F32),
            pltpu.VMEM((SUBLANES + chunk, conv_ch), F32),
            pltpu.VMEM((chunk, d_inner), F32),
        ],
        compiler_params=_cparams("parallel", "arbitrary"),
        name="ssd_core",
    )(zx, dt_raw, cw, dt_bias, a_log, d_full, norm_w, expand)


def _ssd_layer(x, norm_w, w_in, conv_w, conv_b, dt_bias, a_log, d_skip, ssd_norm, w_out, batch, seq_len):
    heads = d_skip.shape[0]
    d_inner = heads * SSD_HEAD_DIM
    n_zx = w_in.shape[1] - heads
    assert heads <= LANES
    pad = lambda v: jnp.concatenate([v.astype(F32), jnp.zeros((LANES - heads,), F32)]).reshape(1, LANES)
    w_dt = jnp.concatenate([w_in[:, n_zx:], jnp.zeros((w_in.shape[0], LANES - heads), F32)], axis=1).astype(BF16)
    zx = _proj(x, w_in.astype(BF16), cols=(0, n_zx), norm_w=norm_w, name="ssd_in")
    dt_raw = _proj(x, w_dt, norm_w=norm_w, name="ssd_dt")
    expand = (jnp.arange(LANES)[:, None] == (jnp.arange(d_inner) // SSD_HEAD_DIM)[None, :]).astype(BF16)
    d_full = jnp.repeat(d_skip.astype(F32), SSD_HEAD_DIM).reshape(1, d_inner)
    y = _ssd_core(zx, dt_raw, _conv_rows(conv_w, conv_b), pad(dt_bias), pad(a_log), d_full,
                  ssd_norm.reshape(1, d_inner).astype(F32), expand, batch, seq_len)
    return _proj(y, w_out.astype(BF16), res=x, name="ssd_out")


def _ret_body(qk_ref, v_ref, g_ref, o_ref, state_ref, *, heads, dk, dv):
    c = pl.program_id(1)
    L = qk_ref.shape[0]

    @pl.when(c == 0)
    def _():
        state_ref[...] = jnp.zeros_like(state_ref)

    r_i = lax.broadcasted_iota(jnp.int32, (L, L), 0)
    c_i = lax.broadcasted_iota(jnp.int32, (L, L), 1)
    diff = (r_i - c_i).astype(F32)
    lower = diff >= 0
    idx = lax.broadcasted_iota(jnp.int32, (L, 1), 0).astype(F32)

    for h in range(heads):
        log_g = math.log(1.0 - 2.0 ** (-5.0 - h))
        d_intra = jnp.where(lower, jnp.exp(jnp.maximum(diff, 0.0) * log_g), 0.0)
        k_dec = jnp.exp((L - 1.0 - idx) * log_g)
        q_dec = jnp.exp((idx + 1.0) * log_g)
        q = qk_ref[:, h * dk:(h + 1) * dk]
        k = qk_ref[:, (heads + h) * dk:(heads + h + 1) * dk]
        vs = slice(h * dv, (h + 1) * dv)
        v = v_ref[:, vs]
        prev = state_ref[h]
        scores = (_dot_nt(q, k) * d_intra).astype(BF16)
        o = _dot(scores, v) + q_dec * _dot(q, prev.astype(BF16))
        kd_t = (k.astype(F32) * k_dec).T.astype(BF16)
        state_ref[h] = math.exp(L * log_g) * prev + _dot(kd_t, v)
        ms = jnp.mean(o * o, axis=-1, keepdims=True)
        o_ref[:, vs] = (_silu(g_ref[:, vs]) * (o * lax.rsqrt(ms + RMS_EPS))).astype(o_ref.dtype)


def _ret_core(qk, v, g, batch, seq_len, dk, dv, chunk=256):
    T = qk.shape[0]
    H = RET_HEADS
    nc = seq_len // chunk
    blk = lambda n: pl.BlockSpec((chunk, n), lambda b, c: (b * nc + c, 0))
    return pl.pallas_call(
        functools.partial(_ret_body, heads=H, dk=dk, dv=dv),
        grid=(batch, nc),
        in_specs=[blk(2 * H * dk), blk(H * dv), blk(H * dv)],
        out_specs=blk(H * dv),
        out_shape=jax.ShapeDtypeStruct((T, H * dv), BF16),
        scratch_shapes=[pltpu.VMEM((H, dk, dv), F32)],
        compiler_params=_cparams("parallel", "arbitrary"),
        name="ret_core",
    )(qk, v, g)


def _ret_layer(x, norm_w, tabs, w_in, wo, batch, seq_len):
    cos_r, sin_r = tabs[2], tabs[3]
    D = x.shape[1]
    H = RET_HEADS
    dk = D // H
    dv = 2 * D // H
    n_qk = 2 * H * dk
    n_v = H * dv
    n_all = w_in.shape[1]
    col = jnp.arange(n_all)
    col_scale = jnp.where((col >= H * dk) & (col < n_qk), dk ** -0.5, 1.0).astype(F32)
    w = (w_in * col_scale[None, :]).astype(BF16)
    qk = _proj(x, w, cols=(0, n_qk), norm_w=norm_w, epi="rope_half", cos=cos_r, sin=sin_r, out_dtype=BF16,
               name="ret_qk")
    v = _proj(x, w, cols=(n_qk, n_v), norm_w=norm_w, out_dtype=BF16, name="ret_v")
    g = _proj(x, w, cols=(n_qk + n_v, n_all - n_qk - n_v), norm_w=norm_w, name="ret_g")
    o = _ret_core(qk, v, g, batch, seq_len, dk, dv)
    return _proj(o, wo.astype(BF16), res=x, name="ret_out")


def kernel(x, positions, norm_mix, norm_ffn, norm_final, mla_wq_a, mla_q_norm, mla_wq_b, mla_wkv_a, mla_kv_norm, mla_wkv_b, mla_wo, sb_wqkv, sb_wo, ssd_w_in, ssd_conv_w, ssd_conv_b, ssd_dt_bias, ssd_a_log, ssd_d, ssd_norm, ssd_w_out, ret_w_in, ret_wo, ffn_w_up, ffn_conv_w, ffn_conv_b, ffn_w_down):
    B, S, D = x.shape
    depth = norm_mix.shape[0]
    n_mixers = 4
    tabs = _rope_tables(positions, D // RET_HEADS)
    w_up = ffn_w_up.astype(BF16)
    w_down = ffn_w_down.astype(BF16)
    cw = _conv_rows(ffn_conv_w, ffn_conv_b)
    xt = x.reshape(B * S, D)
    for i in range(depth):
        m, j = i % n_mixers, i // n_mixers
        if m == 0:
            xt = _mla_layer(xt, norm_mix[i], tabs, mla_wq_a[j], mla_q_norm[j], mla_wq_b[j], mla_wkv_a[j],
                            mla_kv_norm[j], mla_wkv_b[j], mla_wo[j], B, S)
        elif m == 1:
            xt = _sb_layer(xt, norm_mix[i], sb_wqkv[j], sb_wo[j], B, S)
        elif m == 2:
            xt = _ssd_layer(xt, norm_mix[i], ssd_w_in[j], ssd_conv_w[j], ssd_conv_b[j], ssd_dt_bias[j],
                            ssd_a_log[j], ssd_d[j], ssd_norm[j], ssd_w_out[j], B, S)
        else:
            xt = _ret_layer(xt, norm_mix[i], tabs, ret_w_in[j], ret_wo[j], B, S)
        xt = _ffn(xt, norm_ffn[i], w_up, cw, w_down, i, S)
    return _rmsnorm(xt, norm_final).reshape(B, S, D)
```

```python
import functools
import math

import jax
import jax.numpy as jnp
from jax import lax
from jax.experimental import pallas as pl
from jax.experimental.pallas import tpu as pltpu

F32 = jnp.float32
BF16 = jnp.bfloat16

RMS_EPS = 1e-6
ROPE_BASE = 10000.0
LOG2_E = 1.4426950408889634
SB_DEAD_LOG2 = 160.0

MLA_HEADS = 16
MLA_Q_RANK = 512
MLA_KV_RANK = 512
MLA_NOPE = 128
MLA_ROPE = 64
MLA_V = 128
SB_HEADS = 16
SB_HEAD_DIM = 128
SSD_HEAD_DIM = 64
SSD_GROUPS = 8
SSD_STATE = 128
SSD_CONV = 4
RET_HEADS = 8
FFN_CONV = 3

LANES = 128
SUBLANES = 8
VMEM_LIMIT_BYTES = 56 * 1024 * 1024
PROJ_VMEM_BUDGET_BYTES = 44 * 1024 * 1024


def _cparams(*sem):
    return pltpu.CompilerParams(dimension_semantics=sem, vmem_limit_bytes=VMEM_LIMIT_BYTES)


def _dot(a, b):
    return jnp.dot(a, b, preferred_element_type=F32)


def _dot_nt(a, b):
    return lax.dot_general(a, b, (((1,), (1,)), ((), ())), preferred_element_type=F32)


def _dot_split(a, b):
    hi = a.astype(BF16)
    lo = (a - hi.astype(F32)).astype(BF16)
    return _dot(hi, b) + _dot(lo, b)


def _rms_rows(xf, w):
    ms = jnp.mean(xf * xf, axis=-1, keepdims=True)
    return xf * lax.rsqrt(ms + RMS_EPS) * w


def _silu(x):
    return x * jax.nn.sigmoid(x)


def _neg_abs(x):
    bits = lax.bitcast_convert_type(x, jnp.uint32) | jnp.uint32(0x80000000)
    return lax.bitcast_convert_type(bits, F32)


def _softplus(x):
    return jnp.maximum(x, 0.0) + jnp.log(1.0 + jnp.exp(-jnp.abs(x)))


def _rope_tables_body(pos_ref, inv_a_ref, sgn_a_ref, inv_r_ref, cos_a_ref, sin_a_ref, cos_r_ref, sin_r_ref):
    pos = pos_ref[...]
    ang_a = pos * inv_a_ref[...]
    cos_a_ref[...] = jnp.cos(ang_a)
    sin_a_ref[...] = jnp.sin(ang_a) * sgn_a_ref[...]
    ang_r = pos * inv_r_ref[...]
    cos_r_ref[...] = jnp.cos(ang_r)
    sin_r_ref[...] = jnp.sin(ang_r)


def _rope_tables(positions, ret_dk):
    T = positions.size
    tm = 1024
    pos = positions.reshape(T, 1).astype(F32)
    half_a = MLA_ROPE // 2
    inv_a = 1.0 / (ROPE_BASE ** (jnp.arange(half_a, dtype=F32) * (2.0 / MLA_ROPE)))
    z = jnp.zeros((half_a,), F32)
    inv_a = jnp.concatenate([inv_a, z, inv_a, z]).reshape(1, LANES)
    o = jnp.ones((2 * half_a,), F32)
    sgn_a = jnp.concatenate([-o, o]).reshape(1, LANES)
    half_r = ret_dk // 2
    assert half_r == LANES
    inv_r = (1.0 / (ROPE_BASE ** (jnp.arange(half_r, dtype=F32) * (2.0 / ret_dk)))).reshape(1, LANES)
    row = pl.BlockSpec((1, LANES), lambda i: (0, 0))
    tab = pl.BlockSpec((tm, LANES), lambda i: (i, 0))
    shp = jax.ShapeDtypeStruct((T, LANES), F32)
    return pl.pallas_call(
        _rope_tables_body,
        grid=(T // tm,),
        in_specs=[pl.BlockSpec((tm, 1), lambda i: (i, 0)), row, row, row],
        out_specs=[tab, tab, tab, tab],
        out_shape=[shp, shp, shp, shp],
        compiler_params=_cparams("parallel"),
        name="rope_tables",
    )(pos, inv_a, sgn_a, inv_r)


def _proj_body(*refs, has_norm, epi, has_res):
    it = iter(refs)
    a_ref = next(it)
    nw_ref = next(it) if has_norm else None
    w_ref = next(it)
    res_ref = next(it) if has_res else None
    cos_ref = sin_ref = None
    if epi is not None:
        cos_ref = next(it)
        sin_ref = next(it)
    o_ref = next(it)
    abf_ref = next(it) if has_norm else None

    if has_norm:
        @pl.when(pl.program_id(1) == 0)
        def _():
            abf_ref[...] = _rms_rows(a_ref[...].astype(F32), nw_ref[...]).astype(BF16)
        a = abf_ref[...]
    else:
        a = a_ref[...]
    acc = _dot(a, w_ref[...])
    if epi == "rope_half":
        c = cos_ref[...]
        s = sin_ref[...]
        parts = []
        for j in range(acc.shape[1] // (2 * LANES)):
            x1 = acc[:, (2 * j) * LANES:(2 * j + 1) * LANES]
            x2 = acc[:, (2 * j + 1) * LANES:(2 * j + 2) * LANES]
            parts += [x1 * c - x2 * s, x2 * c + x1 * s]
        acc = jnp.concatenate(parts, axis=1)
    elif epi == "rope_group":
        c = cos_ref[...]
        s = sin_ref[...]
        parts = []
        for j in range(acc.shape[1] // (2 * LANES)):
            g = acc[:, (2 * j + 1) * LANES:(2 * j + 2) * LANES]
            parts += [acc[:, (2 * j) * LANES:(2 * j + 1) * LANES], g * c + pltpu.roll(g, LANES // 2, 1) * s]
        acc = jnp.concatenate(parts, axis=1)
    if has_res:
        acc = acc + res_ref[...]
    o_ref[...] = acc.astype(o_ref.dtype)


def _proj_tiles(T, K, N, a_bytes, out_bytes, has_norm, has_res):
    for tm in (1024, 512, 256):
        if T % tm:
            continue
        for tn in (1024, 512, 256, LANES):
            if N % tn:
                continue
            need = (2 * tm * K * a_bytes + (tm * K * 2 if has_norm else 0) + 2 * K * tn * 2
                    + 2 * tm * tn * out_bytes + (2 * tm * tn * 4 if has_res else 0) + tm * tn * 4)
            if need <= PROJ_VMEM_BUDGET_BYTES:
                return tm, tn
    raise ValueError("no projection tiling fits VMEM")


def _proj(a, w, *, cols=None, norm_w=None, res=None, epi=None, cos=None, sin=None, out_dtype=F32, name="proj"):
    T, K = a.shape
    c0, N = (0, w.shape[1]) if cols is None else cols
    has_norm = norm_w is not None
    has_res = res is not None
    tm, tn = _proj_tiles(T, K, N, a.dtype.itemsize, jnp.dtype(out_dtype).itemsize, has_norm, has_res)
    assert c0 % tn == 0
    nb0 = c0 // tn
    in_specs = [pl.BlockSpec((tm, K), lambda m, n: (m, 0))]
    args = [a]
    if has_norm:
        in_specs.append(pl.BlockSpec((1, K), lambda m, n: (0, 0)))
        args.append(norm_w.reshape(1, K).astype(F32))
    in_specs.append(pl.BlockSpec((K, tn), lambda m, n: (0, nb0 + n)))
    args.append(w)
    if has_res:
        in_specs.append(pl.BlockSpec((tm, tn), lambda m, n: (m, n)))
        args.append(res)
    if epi is not None:
        in_specs += [pl.BlockSpec((tm, LANES), lambda m, n: (m, 0))] * 2
        args += [cos, sin]
    scratch = [pltpu.VMEM((tm, K), BF16)] if has_norm else []
    return pl.pallas_call(
        functools.partial(_proj_body, has_norm=has_norm, epi=epi, has_res=has_res),
        grid=(T // tm, N // tn),
        in_specs=in_specs,
        out_specs=pl.BlockSpec((tm, tn), lambda m, n: (m, n)),
        out_shape=jax.ShapeDtypeStruct((T, N), out_dtype),
        scratch_shapes=scratch,
        compiler_params=_cparams("parallel", "arbitrary"),
        name=name,
    )(*args)


def _rmsnorm_body(x_ref, w_ref, o_ref):
    o_ref[...] = _rms_rows(x_ref[...], w_ref[...])


def _rmsnorm(x, w, tm=512):
    T, D = x.shape
    return pl.pallas_call(
        _rmsnorm_body,
        grid=(T // tm,),
        in_specs=[pl.BlockSpec((tm, D), lambda m: (m, 0)), pl.BlockSpec((1, D), lambda m: (0, 0))],
        out_specs=pl.BlockSpec((tm, D), lambda m: (m, 0)),
        out_shape=jax.ShapeDtypeStruct((T, D), F32),
        compiler_params=_cparams("parallel"),
        name="final_norm",
    )(x, w.reshape(1, D).astype(F32))


def _ffn_body(x_ref, nw_ref, wg_ref, wu_ref, cg_ref, cu_ref, cgp_ref, cup_ref, wd_ref, xres_ref, o_ref,
              h_ref, act_ref, ag_ref, au_ref, bg_ref, bu_ref, carry_g_ref, carry_u_ref, *, tiles_per_seq, nf):
    m = pl.program_id(0)
    f = pl.program_id(1)
    tm = x_ref.shape[0]
    tf = wg_ref.shape[1]
    half = tm // 2

    def activation(g_ref, u_ref, cwg, cwu):
        def conv(buf_ref, cw):
            win = buf_ref[...]
            return (cw[0:1, :] * pltpu.roll(win, 2, 0)[SUBLANES:, :]
                    + cw[1:2, :] * pltpu.roll(win, 1, 0)[SUBLANES:, :]
                    + cw[2:3, :] * win[SUBLANES:, :] + cw[3:4, :])
        return (_silu(conv(g_ref, cwg)) * conv(u_ref, cwu)).astype(BF16)

    def act_cols(fi):
        return pl.ds(pl.multiple_of(fi * tf, tf), tf)

    @pl.when(f == 0)
    def _():
        h_ref[...] = _rms_rows(x_ref[...], nw_ref[...]).astype(BF16)

    @pl.when(jnp.logical_and(f == 0, m == 0))
    def _():
        carry_g_ref[...] = jnp.zeros_like(carry_g_ref)
        carry_u_ref[...] = jnp.zeros_like(carry_u_ref)
        bg_ref[...] = jnp.zeros_like(bg_ref)
        bu_ref[...] = jnp.zeros_like(bu_ref)

    def finish_previous(fi):
        act_ref[0:half, act_cols(fi)] = activation(ag_ref, au_ref, cgp_ref[...], cup_ref[...])
        act_ref[half:, act_cols(fi)] = activation(bg_ref, bu_ref, cgp_ref[...], cup_ref[...])

    @pl.when(f < nf)
    def _up():
        finish_previous(jnp.maximum(f - 1, 0))
        seq_start = m % tiles_per_seq == 0
        ag_ref[0:SUBLANES, :] = jnp.where(seq_start, 0.0, carry_g_ref[f])
        au_ref[0:SUBLANES, :] = jnp.where(seq_start, 0.0, carry_u_ref[f])
        h_a = h_ref[0:half, :]
        ag_ref[SUBLANES:, :] = _dot(h_a, wg_ref[...])
        au_ref[SUBLANES:, :] = _dot(h_a, wu_ref[...])
        h_b = h_ref[half:, :]
        bg_ref[0:SUBLANES, :] = ag_ref[half:half + SUBLANES, :]
        bu_ref[0:SUBLANES, :] = au_ref[half:half + SUBLANES, :]
        bg_ref[SUBLANES:, :] = _dot(h_b, wg_ref[...])
        bu_ref[SUBLANES:, :] = _dot(h_b, wu_ref[...])
        carry_g_ref[f] = bg_ref[half:half + SUBLANES, :]
        carry_u_ref[f] = bu_ref[half:half + SUBLANES, :]

    @pl.when(f == nf)
    def _():
        finish_previous(nf - 1)

    @pl.when(f >= nf)
    def _down():
        o_ref[...] = xres_ref[...] + _dot(act_ref[...], wd_ref[...])


def _ffn(x, norm_w, w_up, cw, w_down, layer, seq_len, tm=1024, tf=512, td=256):
    T, D = x.shape
    F = w_down.shape[1]
    assert T % tm == 0 and F % tf == 0 and seq_len % tm == 0 and D % td == 0 and (tm // 2) % SUBLANES == 0
    nf = F // tf
    nd = D // td
    cur = lambda f: jnp.minimum(f, nf - 1)
    prev = lambda f: jnp.maximum(jnp.minimum(f, nf) - 1, 0)
    gate_blk = lambda m, f: (layer, 0, cur(f))
    up_blk = lambda m, f: (layer, 0, nf + cur(f))
    gate_prev = lambda m, f: (layer, 0, prev(f))
    up_prev = lambda m, f: (layer, 0, nf + prev(f))
    down_blk = lambda m, f: (layer, 0, jnp.maximum(f - nf, 0))
    out_blk = lambda m, f: (m, jnp.maximum(f - nf, 0))
    stage = pltpu.VMEM((SUBLANES + tm // 2, tf), F32)
    return pl.pallas_call(
        functools.partial(_ffn_body, tiles_per_seq=seq_len // tm, nf=nf),
        grid=(T // tm, nf + nd),
        in_specs=[
            pl.BlockSpec((tm, D), lambda m, f: (m, 0)),
            pl.BlockSpec((1, D), lambda m, f: (0, 0)),
            pl.BlockSpec((None, D, tf), gate_blk),
            pl.BlockSpec((None, D, tf), up_blk),
            pl.BlockSpec((None, SUBLANES, tf), gate_blk),
            pl.BlockSpec((None, SUBLANES, tf), up_blk),
            pl.BlockSpec((None, SUBLANES, tf), gate_prev),
            pl.BlockSpec((None, SUBLANES, tf), up_prev),
            pl.BlockSpec((None, F, td), down_blk),
            pl.BlockSpec((tm, td), out_blk),
        ],
        out_specs=pl.BlockSpec((tm, td), out_blk),
        out_shape=jax.ShapeDtypeStruct((T, D), F32),
        scratch_shapes=[
            pltpu.VMEM((tm, D), BF16),
            pltpu.VMEM((tm, F), BF16),
            stage, stage, stage, stage,
            pltpu.VMEM((nf, SUBLANES, tf), F32),
            pltpu.VMEM((nf, SUBLANES, tf), F32),
        ],
        compiler_params=_cparams("arbitrary", "arbitrary"),
        name="conv_ffn",
    )(x, norm_w.reshape(1, D).astype(F32), w_up, w_up, cw, cw, cw, cw, w_down, x)


def _conv_rows(w, b):
    K, C = w.shape[-2:]
    pad = jnp.zeros(w.shape[:-2] + (SUBLANES - K - 1, C), F32)
    return jnp.concatenate([w.astype(F32), b.astype(F32)[..., None, :], pad], axis=-2)


def _mla_down_body(x_ref, nw_ref, w_ref, qn_ref, kvn_ref, cos_ref, sin_ref, cq_ref, ckv_ref, kpe_ref):
    h = _rms_rows(x_ref[...], nw_ref[...]).astype(BF16)
    r = _dot(h, w_ref[...])
    cq_ref[...] = _rms_rows(r[:, :MLA_Q_RANK], qn_ref[...]).astype(BF16)
    ckv_ref[...] = _rms_rows(r[:, MLA_Q_RANK:MLA_Q_RANK + MLA_KV_RANK], kvn_ref[...]).astype(BF16)
    g = r[:, MLA_Q_RANK + MLA_KV_RANK:]
    kpe_ref[...] = (g * cos_ref[...] + pltpu.roll(g, LANES // 2, 1) * sin_ref[...]).astype(BF16)


def _mla_down(x, norm_w, w, q_norm, kv_norm, cos, sin, tm=512):
    T, D = x.shape
    N = w.shape[1]
    row = lambda n: pl.BlockSpec((1, n), lambda m: (0, 0))
    tile = lambda n: pl.BlockSpec((tm, n), lambda m: (m, 0))
    return pl.pallas_call(
        _mla_down_body,
        grid=(T // tm,),
        in_specs=[tile(D), row(D), pl.BlockSpec((D, N), lambda m: (0, 0)), row(MLA_Q_RANK), row(MLA_KV_RANK),
                  tile(LANES), tile(LANES)],
        out_specs=[tile(MLA_Q_RANK), tile(MLA_KV_RANK), tile(LANES)],
        out_shape=[jax.ShapeDtypeStruct((T, MLA_Q_RANK), BF16), jax.ShapeDtypeStruct((T, MLA_KV_RANK), BF16),
                   jax.ShapeDtypeStruct((T, LANES), BF16)],
        compiler_params=_cparams("parallel"),
        name="mla_down",
    )(x, norm_w.reshape(1, D).astype(F32), w, q_norm.reshape(1, -1).astype(F32),
      kv_norm.reshape(1, -1).astype(F32), cos, sin)


def _lane_tile(x, width):
    return jnp.concatenate([x] * (width // LANES), axis=1)


def _mla_attn_body(q_ref, kn_ref, kpe_ref, v_ref, o_ref, kcat_ref, m_ref, l_ref, acc_ref, *, tk, hp):
    i = pl.program_id(2)
    tq = q_ref.shape[0]

    @pl.when(i == 0)
    def _():
        for h in range(hp):
            kcat_ref[h, :, :LANES] = kn_ref[:, h * LANES:(h + 1) * LANES]
            kcat_ref[h, :, LANES:] = kpe_ref[...]

    m_ref[...] = jnp.full(m_ref.shape, -jnp.inf, F32)
    l_ref[...] = jnp.zeros_like(l_ref)
    acc_ref[...] = jnp.zeros_like(acc_ref)
    qs = [q_ref[:, h * 2 * LANES:(h + 1) * 2 * LANES] for h in range(hp)]

    def step(j, masked):
        ks = pl.ds(pl.multiple_of(j * tk, tk), tk)
        ss = [_dot_nt(qs[h], kcat_ref[h, ks, :]) for h in range(hp)]
        for h in range(hp):
            s = ss[h]
            if masked:
                row = lax.broadcasted_iota(jnp.int32, (tq, tk), 0)
                col = lax.broadcasted_iota(jnp.int32, (tq, tk), 1)
                s = jnp.where(col <= row, s, -jnp.inf)
            m_prev = m_ref[h]
            m_new = jnp.maximum(m_prev, jnp.max(s, axis=-1, keepdims=True))
            alpha = jnp.exp2(m_prev - m_new)
            p = jnp.exp2(s - _lane_tile(m_new, tk))
            l_ref[h] = alpha * l_ref[h] + jnp.sum(p, axis=-1, keepdims=True)
            acc_ref[h] = alpha * acc_ref[h] + _dot(p.astype(BF16), v_ref[ks, h * LANES:(h + 1) * LANES])
            m_ref[h] = m_new

    def body(j, c):
        step(j, False)
        return c

    lax.fori_loop(0, i, body, 0)
    step(i, True)
    for h in range(hp):
        o_ref[:, h * LANES:(h + 1) * LANES] = (acc_ref[h] / l_ref[h]).astype(o_ref.dtype)


def _mla_attn(q, kv, kpe, batch, seq_len, tq=512, hp=4):
    T = q.shape[0]
    H = MLA_HEADS
    assert MLA_V == LANES and MLA_NOPE == LANES and H % hp == 0
    nq = seq_len // tq
    ng = H // hp
    return pl.pallas_call(
        functools.partial(_mla_attn_body, tk=tq, hp=hp),
        grid=(batch, ng, nq),
        in_specs=[
            pl.BlockSpec((tq, hp * 2 * LANES), lambda b, g, i: (b * nq + i, g)),
            pl.BlockSpec((seq_len, hp * LANES), lambda b, g, i: (b, g)),
            pl.BlockSpec((seq_len, LANES), lambda b, g, i: (b, 0)),
            pl.BlockSpec((seq_len, hp * LANES), lambda b, g, i: (b, ng + g)),
        ],
        out_specs=pl.BlockSpec((tq, hp * LANES), lambda b, g, i: (b * nq + i, g)),
        out_shape=jax.ShapeDtypeStruct((T, H * MLA_V), BF16),
        scratch_shapes=[
            pltpu.VMEM((hp, seq_len, 2 * LANES), BF16),
            pltpu.VMEM((hp, tq, LANES), F32),
            pltpu.VMEM((hp, tq, LANES), F32),
            pltpu.VMEM((hp, tq, LANES), F32),
        ],
        compiler_params=_cparams("parallel", "parallel", "arbitrary"),
        name="mla_attn",
    )(q, kv, kpe, kv)


def _mla_layer(x, norm_w, tabs, wq_a, q_norm, wq_b, wkv_a, kv_norm, wkv_b, wo, batch, seq_len):
    cos_a, sin_a = tabs[0], tabs[1]
    D = x.shape[1]
    H = MLA_HEADS
    half = MLA_ROPE // 2
    scale = (MLA_NOPE + MLA_ROPE) ** -0.5 * LOG2_E
    zk = jnp.zeros((D, half), F32)
    w_down = jnp.concatenate(
        [wq_a, wkv_a[:, :MLA_KV_RANK], wkv_a[:, MLA_KV_RANK:MLA_KV_RANK + half], zk,
         wkv_a[:, MLA_KV_RANK + half:], zk], axis=1).astype(BF16)
    wq = wq_b.reshape(MLA_Q_RANK, H, MLA_NOPE + MLA_ROPE)
    zq = jnp.zeros((MLA_Q_RANK, H, half), F32)
    wq = jnp.concatenate([wq[:, :, :MLA_NOPE], wq[:, :, MLA_NOPE:MLA_NOPE + half], zq,
                          wq[:, :, MLA_NOPE + half:], zq], axis=2)
    wq = (wq * scale).reshape(MLA_Q_RANK, H * 2 * LANES).astype(BF16)
    wkv = wkv_b.reshape(MLA_KV_RANK, H, 2, MLA_NOPE).transpose(0, 2, 1, 3).reshape(MLA_KV_RANK, 2 * H * MLA_NOPE)
    wkv = wkv.astype(BF16)

    cq, ckv, kpe = _mla_down(x, norm_w, w_down, q_norm, kv_norm, cos_a, sin_a)
    q = _proj(cq, wq, epi="rope_group", cos=cos_a, sin=sin_a, out_dtype=BF16, name="mla_q")
    kv = _proj(ckv, wkv, out_dtype=BF16, name="mla_kv")
    o = _mla_attn(q, kv, kpe, batch, seq_len)
    return _proj(o, wo.astype(BF16), res=x, name="mla_out")


def _sb_attn_body(q_ref, k_ref, v_ref, after_ref, o_ref, acc_ref, drop_ref, *, tk, sub, hp):
    i = pl.program_id(2)
    tq = q_ref.shape[0]
    acc_ref[...] = jnp.zeros_like(acc_ref)
    drop_ref[...] = jnp.zeros_like(drop_ref)
    qs = [q_ref[:, h * LANES:(h + 1) * LANES] for h in range(hp)]
    after = after_ref[...]

    def step(j, masked):
        ks = pl.ds(pl.multiple_of(j * tk, tk), tk)
        zs = [_dot_nt(qs[h], k_ref[ks, h * LANES:(h + 1) * LANES]) for h in range(hp)]
        for h in range(hp):
            z = zs[h]
            sp = jnp.maximum(z, 0.0) + jnp.log2(1.0 + jnp.exp2(_neg_abs(z)))
            if masked:
                row = lax.broadcasted_iota(jnp.int32, (tq, tk), 0)
                col = lax.broadcasted_iota(jnp.int32, (tq, tk), 1)
                strict = col < row
                sp = jnp.where(strict, sp, 0.0)
            drop = drop_ref[h]
            parts = [None] * (tk // sub)
            for b in reversed(range(tk // sub)):
                cs = slice(b * sub, (b + 1) * sub)
                sp_b = sp[:, cs]
                within = _dot(sp_b.astype(BF16), after)
                a_b = jnp.exp2(z[:, cs] - sp_b - within - _lane_tile(drop, sub))
                if masked:
                    a_b = jnp.where(strict[:, cs], a_b, 0.0)
                parts[b] = a_b.astype(BF16)
                drop = drop + jnp.sum(sp_b, axis=-1, keepdims=True)
            acc_ref[h] += _dot(jnp.concatenate(parts, axis=1), v_ref[ks, h * LANES:(h + 1) * LANES])
            drop_ref[h] = drop

    step(i, True)

    def live():
        return jnp.min(drop_ref[...]) < SB_DEAD_LOG2

    def body(carry):
        t, _ = carry
        step(i - 1 - t, False)
        return t + 1, live()

    lax.while_loop(lambda c: jnp.logical_and(c[0] < i, c[1]), body, (jnp.int32(0), live()))
    for h in range(hp):
        o_ref[:, h * LANES:(h + 1) * LANES] = acc_ref[h].astype(o_ref.dtype)


def _sb_attn(qkv, batch, seq_len, tq=512, sub=256, hp=2):
    T = qkv.shape[0]
    H, Dh = SB_HEADS, SB_HEAD_DIM
    assert Dh == LANES and H % hp == 0
    nq = seq_len // tq
    ng = H // hp
    after = (jnp.arange(sub)[:, None] > jnp.arange(sub)[None, :]).astype(BF16)
    return pl.pallas_call(
        functools.partial(_sb_attn_body, tk=tq, sub=sub, hp=hp),
        grid=(batch, ng, nq),
        in_specs=[
            pl.BlockSpec((tq, hp * Dh), lambda b, g, i: (b * nq + i, g)),
            pl.BlockSpec((seq_len, hp * Dh), lambda b, g, i: (b, ng + g)),
            pl.BlockSpec((seq_len, hp * Dh), lambda b, g, i: (b, 2 * ng + g)),
            pl.BlockSpec((sub, sub), lambda b, g, i: (0, 0)),
        ],
        out_specs=pl.BlockSpec((tq, hp * Dh), lambda b, g, i: (b * nq + i, g)),
        out_shape=jax.ShapeDtypeStruct((T, H * Dh), BF16),
        scratch_shapes=[pltpu.VMEM((hp, tq, Dh), F32), pltpu.VMEM((hp, tq, LANES), F32)],
        compiler_params=_cparams("parallel", "parallel", "arbitrary"),
        name="sb_attn",
    )(qkv, qkv, qkv, after)


def _sb_layer(x, norm_w, wqkv, wo, batch, seq_len):
    n_q = SB_HEADS * SB_HEAD_DIM
    q_scale = SB_HEAD_DIM ** -0.5 * LOG2_E
    col_scale = jnp.where(jnp.arange(wqkv.shape[1]) < n_q, q_scale, 1.0).astype(F32)
    w = (wqkv * col_scale[None, :]).astype(BF16)
    qkv = _proj(x, w, norm_w=norm_w, out_dtype=BF16, name="sb_qkv")
    o = _sb_attn(qkv, batch, seq_len)
    return _proj(o, wo.astype(BF16), res=x, name="sb_out")


def _ssd_body(zx_ref, dt_ref, cw_ref, dtb_ref, alog_ref, dfull_ref, nw_ref, exp_ref, o_ref,
              state_ref, carry_ref, buf_ref, y_ref, *, d_inner):
    c = pl.program_id(1)
    L = zx_ref.shape[0]
    G, N, P = SSD_GROUPS, SSD_STATE, SSD_HEAD_DIM
    gw = d_inner // G
    conv_ch = d_inner + 2 * G * N

    @pl.when(c == 0)
    def _():
        state_ref[...] = jnp.zeros_like(state_ref)
        carry_ref[...] = jnp.zeros_like(carry_ref)

    u = zx_ref[:, d_inner:d_inner + conv_ch]
    buf_ref[0:SUBLANES, :] = carry_ref[...]
    buf_ref[SUBLANES:SUBLANES + L, :] = u
    carry_ref[...] = u[L - SUBLANES:L, :]
    cw = cw_ref[...]
    conv = cw[SSD_CONV:SSD_CONV + 1, :] + cw[SSD_CONV - 1:SSD_CONV, :] * u
    win = buf_ref[...]
    for k in range(SSD_CONV - 1):
        conv = conv + cw[k:k + 1, :] * pltpu.roll(win, SSD_CONV - 1 - k, 0)[SUBLANES:, :]
    xbc = _silu(conv)
    xs = xbc[:, :d_inner]
    bm = xbc[:, d_inner:d_inner + G * N]
    cm = xbc[:, d_inner + G * N:]

    dt = _softplus(dt_ref[...] + dtb_ref[...])
    a = dt * (-jnp.exp(alog_ref[...]))
    r_i = lax.broadcasted_iota(jnp.int32, (L, L), 0)
    c_i = lax.broadcasted_iota(jnp.int32, (L, L), 1)
    causal = c_i <= r_i
    acum_t = _dot_split(a.T, jnp.where(c_i >= r_i, 1.0, 0.0).astype(BF16))
    acum = acum_t.T
    expand = exp_ref[...]
    dt_full = _dot_split(dt, expand)
    acum_full = _dot_split(acum, expand)
    last_full = acum_full[L - 1:L, :]
    xdt = xs * dt_full
    xdt_bf = xdt.astype(BF16)
    xdec_bf = (xdt * jnp.exp(last_full - acum_full)).astype(BF16)
    grow = jnp.exp(acum_full)
    chunk_decay = jnp.exp(last_full)
    lane = lax.broadcasted_iota(jnp.int32, (L, LANES), 1)
    first_half = lane < P

    for g in range(G):
        b_g = bm[:, g * N:(g + 1) * N]
        c_g = cm[:, g * N:(g + 1) * N].astype(BF16)
        cb = _dot_nt(c_g, b_g.astype(BF16))
        prev = state_ref[g]
        gs = slice(g * gw, (g + 1) * gw)
        y_off = _dot(c_g, prev.astype(BF16)) * grow[:, gs]
        parts = []
        for pair in range(gw // LANES):
            h0 = g * (gw // P) + 2 * pair
            xp = xdt_bf[:, h0 * P:h0 * P + LANES]
            ms = []
            for h in (h0, h0 + 1):
                seg = acum[:, h:h + 1] - acum_t[h:h + 1, :]
                ms.append((cb * jnp.exp(jnp.where(causal, seg, -jnp.inf))).astype(BF16))
            zero = jnp.zeros_like(xp)
            parts.append(_dot(ms[0], jnp.where(first_half, xp, zero)) + _dot(ms[1], jnp.where(first_half, zero, xp)))
        y_ref[:, gs] = jnp.concatenate(parts, axis=1) + y_off
        state_ref[g] = prev * chunk_decay[:, gs] + _dot(b_g.T.astype(BF16), xdec_bf[:, gs])

    y = (y_ref[...] + dfull_ref[...] * xs) * _silu(zx_ref[:, :d_inner])
    nw = nw_ref[...]
    for g in range(G):
        gs = slice(g * gw, (g + 1) * gw)
        o_ref[:, gs] = _rms_rows(y[:, gs], nw[:, gs]).astype(o_ref.dtype)


def _ssd_core(zx, dt_raw, cw, dt_bias, a_log, d_full, norm_w, expand, batch, seq_len, chunk=128):
    T = zx.shape[0]
    d_inner = d_full.shape[1]
    conv_ch = cw.shape[1]
    nc = seq_len // chunk
    G, N = SSD_GROUPS, SSD_STATE
    row = lambda n: pl.BlockSpec((1, n), lambda b, c: (0, 0))
    return pl.pallas_call(
        functools.partial(_ssd_body, d_inner=d_inner),
        grid=(batch, nc),
        in_specs=[
            pl.BlockSpec((chunk, zx.shape[1]), lambda b, c: (b * nc + c, 0)),
            pl.BlockSpec((chunk, LANES), lambda b, c: (b * nc + c, 0)),
            pl.BlockSpec((SUBLANES, conv_ch), lambda b, c: (0, 0)),
            row(LANES), row(LANES), row(d_inner), row(d_inner),
            pl.BlockSpec((LANES, d_inner), lambda b, c: (0, 0)),
        ],
        out_specs=pl.BlockSpec((chunk, d_inner), lambda b, c: (b * nc + c, 0)),
        out_shape=jax.ShapeDtypeStruct((T, d_inner), BF16),
        scratch_shapes=[
            pltpu.VMEM((G, N, d_inner // G), F32),
            pltpu.VMEM((SUBLANES, conv_ch), F32),
            pltpu.VMEM((SUBLANES + chunk, conv_ch), F32),
            pltpu.VMEM((chunk, d_inner), F32),
        ],
        compiler_params=_cparams("parallel", "arbitrary"),
        name="ssd_core",
    )(zx, dt_raw, cw, dt_bias, a_log, d_full, norm_w, expand)


def _ssd_layer(x, norm_w, w_in, conv_w, conv_b, dt_bias, a_log, d_skip, ssd_norm, w_out, batch, seq_len):
    heads = d_skip.shape[0]
    d_inner = heads * SSD_HEAD_DIM
    n_zx = w_in.shape[1] - heads
    assert heads <= LANES
    pad = lambda v: jnp.concatenate([v.astype(F32), jnp.zeros((LANES - heads,), F32)]).reshape(1, LANES)
    w_dt = jnp.concatenate([w_in[:, n_zx:], jnp.zeros((w_in.shape[0], LANES - heads), F32)], axis=1).astype(BF16)
    zx = _proj(x, w_in.astype(BF16), cols=(0, n_zx), norm_w=norm_w, name="ssd_in")
    dt_raw = _proj(x, w_dt, norm_w=norm_w, name="ssd_dt")
    expand = (jnp.arange(LANES)[:, None] == (jnp.arange(d_inner) // SSD_HEAD_DIM)[None, :]).astype(BF16)
    d_full = jnp.repeat(d_skip.astype(F32), SSD_HEAD_DIM).reshape(1, d_inner)
    y = _ssd_core(zx, dt_raw, _conv_rows(conv_w, conv_b), pad(dt_bias), pad(a_log), d_full,
                  ssd_norm.reshape(1, d_inner).astype(F32), expand, batch, seq_len)
    return _proj(y, w_out.astype(BF16), res=x, name="ssd_out")


def _ret_body(qk_ref, v_ref, g_ref, o_ref, state_ref, *, heads, dk, dv):
    c = pl.program_id(1)
    L = qk_ref.shape[0]

    @pl.when(c == 0)
    def _():
        state_ref[...] = jnp.zeros_like(state_ref)

    r_i = lax.broadcasted_iota(jnp.int32, (L, L), 0)
    c_i = lax.broadcasted_iota(jnp.int32, (L, L), 1)
    diff = (r_i - c_i).astype(F32)
    lower = diff >= 0
    idx = lax.broadcasted_iota(jnp.int32, (L, 1), 0).astype(F32)

    for h in range(heads):
        log_g = math.log(1.0 - 2.0 ** (-5.0 - h))
        d_intra = jnp.where(lower, jnp.exp(jnp.maximum(diff, 0.0) * log_g), 0.0)
        k_dec = jnp.exp((L - 1.0 - idx) * log_g)
        q_dec = jnp.exp((idx + 1.0) * log_g)
        q = qk_ref[:, h * dk:(h + 1) * dk]
        k = qk_ref[:, (heads + h) * dk:(heads + h + 1) * dk]
        vs = slice(h * dv, (h + 1) * dv)
        v = v_ref[:, vs]
        prev = state_ref[h]
        scores = (_dot_nt(q, k) * d_intra).astype(BF16)
        o = _dot(scores, v) + q_dec * _dot(q, prev.astype(BF16))
        kd_t = (k.astype(F32) * k_dec).T.astype(BF16)
        state_ref[h] = math.exp(L * log_g) * prev + _dot(kd_t, v)
        ms = jnp.mean(o * o, axis=-1, keepdims=True)
        o_ref[:, vs] = (_silu(g_ref[:, vs]) * (o * lax.rsqrt(ms + RMS_EPS))).astype(o_ref.dtype)


def _ret_core(qk, v, g, batch, seq_len, dk, dv, chunk=256):
    T = qk.shape[0]
    H = RET_HEADS
    nc = seq_len // chunk
    blk = lambda n: pl.BlockSpec((chunk, n), lambda b, c: (b * nc + c, 0))
    return pl.pallas_call(
        functools.partial(_ret_body, heads=H, dk=dk, dv=dv),
        grid=(batch, nc),
        in_specs=[blk(2 * H * dk), blk(H * dv), blk(H * dv)],
        out_specs=blk(H * dv),
        out_shape=jax.ShapeDtypeStruct((T, H * dv), BF16),
        scratch_shapes=[pltpu.VMEM((H, dk, dv), F32)],
        compiler_params=_cparams("parallel", "arbitrary"),
        name="ret_core",
    )(qk, v, g)


def _ret_layer(x, norm_w, tabs, w_in, wo, batch, seq_len):
    cos_r, sin_r = tabs[2], tabs[3]
    D = x.shape[1]
    H = RET_HEADS
    dk = D // H
    dv = 2 * D // H
    n_qk = 2 * H * dk
    n_v = H * dv
    n_all = w_in.shape[1]
    col = jnp.arange(n_all)
    col_scale = jnp.where((col >= H * dk) & (col < n_qk), dk ** -0.5, 1.0).astype(F32)
    w = (w_in * col_scale[None, :]).astype(BF16)
    qk = _proj(x, w, cols=(0, n_qk), norm_w=norm_w, epi="rope_half", cos=cos_r, sin=sin_r, out_dtype=BF16,
               name="ret_qk")
    v = _proj(x, w, cols=(n_qk, n_v), norm_w=norm_w, out_dtype=BF16, name="ret_v")
    g = _proj(x, w, cols=(n_qk + n_v, n_all - n_qk - n_v), norm_w=norm_w, name="ret_g")
    o = _ret_core(qk, v, g, batch, seq_len, dk, dv)
    return _proj(o, wo.astype(BF16), res=x, name="ret_out")


def kernel(x, positions, norm_mix, norm_ffn, norm_final, mla_wq_a, mla_q_norm, mla_wq_b, mla_wkv_a, mla_kv_norm, mla_wkv_b, mla_wo, sb_wqkv, sb_wo, ssd_w_in, ssd_conv_w, ssd_conv_b, ssd_dt_bias, ssd_a_log, ssd_d, ssd_norm, ssd_w_out, ret_w_in, ret_wo, ffn_w_up, ffn_conv_w, ffn_conv_b, ffn_w_down):
    B, S, D = x.shape
    depth = norm_mix.shape[0]
    n_mixers = 4
    tabs = _rope_tables(positions, D // RET_HEADS)
    w_up = ffn_w_up.astype(BF16)
    w_down = ffn_w_down.astype(BF16)
    cw = _conv_rows(ffn_conv_w, ffn_conv_b)
    xt = x.reshape(B * S, D)
    for i in range(depth):
        m, j = i % n_mixers, i // n_mixers
        if m == 0:
            xt = _mla_layer(xt, norm_mix[i], tabs, mla_wq_a[j], mla_q_norm[j], mla_wq_b[j], mla_wkv_a[j],
                            mla_kv_norm[j], mla_wkv_b[j], mla_wo[j], B, S)
        elif m == 1:
            xt = _sb_layer(xt, norm_mix[i], sb_wqkv[j], sb_wo[j], B, S)
        elif m == 2:
            xt = _ssd_layer(xt, norm_mix[i], ssd_w_in[j], ssd_conv_w[j], ssd_conv_b[j], ssd_dt_bias[j],
                            ssd_a_log[j], ssd_d[j], ssd_norm[j], ssd_w_out[j], B, S)
        else:
            xt = _ret_layer(xt, norm_mix[i], tabs, ret_w_in[j], ret_wo[j], B, S)
        xt = _ffn(xt, norm_ffn[i], w_up, cw, w_down, i, S)
    return _rmsnorm(xt, norm_final).reshape(B, S, D)
```

```python
import functools
import math

import jax
import jax.numpy as jnp
from jax import lax
from jax.experimental import pallas as pl
from jax.experimental.pallas import tpu as pltpu

F32 = jnp.float32
BF16 = jnp.bfloat16

RMS_EPS = 1e-6
ROPE_BASE = 10000.0
LOG2_E = 1.4426950408889634
SB_DEAD_LOG2 = 160.0

MLA_HEADS = 16
MLA_Q_RANK = 512
MLA_KV_RANK = 512
MLA_NOPE = 128
MLA_ROPE = 64
MLA_V = 128
SB_HEADS = 16
SB_HEAD_DIM = 128
SSD_HEAD_DIM = 64
SSD_GROUPS = 8
SSD_STATE = 128
SSD_CONV = 4
RET_HEADS = 8
FFN_CONV = 3

LANES = 128
SUBLANES = 8
VMEM_LIMIT_BYTES = 56 * 1024 * 1024
PROJ_VMEM_BUDGET_BYTES = 48 * 1024 * 1024


def _cparams(*sem):
    return pltpu.CompilerParams(dimension_semantics=sem, vmem_limit_bytes=VMEM_LIMIT_BYTES)


def _dot(a, b):
    return jnp.dot(a, b, preferred_element_type=F32)


def _dot_nt(a, b):
    return lax.dot_general(a, b, (((1,), (1,)), ((), ())), preferred_element_type=F32)


def _dot_split(a, b):
    hi = a.astype(BF16)
    lo = (a - hi.astype(F32)).astype(BF16)
    return _dot(hi, b) + _dot(lo, b)


def _rms_rows(xf, w):
    ms = jnp.mean(xf * xf, axis=-1, keepdims=True)
    return xf * lax.rsqrt(ms + RMS_EPS) * w


def _silu(x):
    return x * jax.nn.sigmoid(x)


def _neg_abs(x):
    bits = lax.bitcast_convert_type(x, jnp.uint32) | jnp.uint32(0x80000000)
    return lax.bitcast_convert_type(bits, F32)


def _softplus(x):
    return jnp.maximum(x, 0.0) + jnp.log(1.0 + jnp.exp(-jnp.abs(x)))


def _rope_tables_body(pos_ref, inv_a_ref, sgn_a_ref, inv_r_ref, cos_a_ref, sin_a_ref, cos_r_ref, sin_r_ref):
    pos = pos_ref[...]
    ang_a = pos * inv_a_ref[...]
    cos_a_ref[...] = jnp.cos(ang_a)
    sin_a_ref[...] = jnp.sin(ang_a) * sgn_a_ref[...]
    ang_r = pos * inv_r_ref[...]
    cos_r_ref[...] = jnp.cos(ang_r)
    sin_r_ref[...] = jnp.sin(ang_r)


def _rope_tables(positions, ret_dk):
    T = positions.size
    tm = 1024
    pos = positions.reshape(T, 1).astype(F32)
    half_a = MLA_ROPE // 2
    inv_a = 1.0 / (ROPE_BASE ** (jnp.arange(half_a, dtype=F32) * (2.0 / MLA_ROPE)))
    z = jnp.zeros((half_a,), F32)
    inv_a = jnp.concatenate([inv_a, z, inv_a, z]).reshape(1, LANES)
    o = jnp.ones((2 * half_a,), F32)
    sgn_a = jnp.concatenate([-o, o]).reshape(1, LANES)
    half_r = ret_dk // 2
    assert half_r == LANES
    inv_r = (1.0 / (ROPE_BASE ** (jnp.arange(half_r, dtype=F32) * (2.0 / ret_dk)))).reshape(1, LANES)
    row = pl.BlockSpec((1, LANES), lambda i: (0, 0))
    tab = pl.BlockSpec((tm, LANES), lambda i: (i, 0))
    shp = jax.ShapeDtypeStruct((T, LANES), F32)
    return pl.pallas_call(
        _rope_tables_body,
        grid=(T // tm,),
        in_specs=[pl.BlockSpec((tm, 1), lambda i: (i, 0)), row, row, row],
        out_specs=[tab, tab, tab, tab],
        out_shape=[shp, shp, shp, shp],
        compiler_params=_cparams("parallel"),
        name="rope_tables",
    )(pos, inv_a, sgn_a, inv_r)


def _proj_body(*refs, has_norm, epi, has_res, n_casts):
    it = iter(refs)
    a_ref = next(it)
    nw_ref = next(it) if has_norm else None
    w_ref = next(it)
    res_ref = next(it) if has_res else None
    cos_ref = sin_ref = None
    if epi is not None:
        cos_ref = next(it)
        sin_ref = next(it)
    cast_src = [next(it) for _ in range(n_casts)]
    o_ref = next(it)
    cast_dst = [next(it) for _ in range(n_casts)]
    abf_ref = next(it) if has_norm else None

    for src_ref, dst_ref in zip(cast_src, cast_dst):
        dst_ref[...] = src_ref[...].astype(BF16)

    if has_norm:
        @pl.when(pl.program_id(1) == 0)
        def _():
            abf_ref[...] = _rms_rows(a_ref[...].astype(F32), nw_ref[...]).astype(BF16)
        a = abf_ref[...]
    else:
        a = a_ref[...]
    acc = _dot(a, w_ref[...])
    if epi == "rope_half":
        c = cos_ref[...]
        s = sin_ref[...]
        parts = []
        for j in range(acc.shape[1] // (2 * LANES)):
            x1 = acc[:, (2 * j) * LANES:(2 * j + 1) * LANES]
            x2 = acc[:, (2 * j + 1) * LANES:(2 * j + 2) * LANES]
            parts += [x1 * c - x2 * s, x2 * c + x1 * s]
        acc = jnp.concatenate(parts, axis=1)
    elif epi == "rope_group":
        c = cos_ref[...]
        s = sin_ref[...]
        parts = []
        for j in range(acc.shape[1] // (2 * LANES)):
            g = acc[:, (2 * j + 1) * LANES:(2 * j + 2) * LANES]
            parts += [acc[:, (2 * j) * LANES:(2 * j + 1) * LANES], g * c + pltpu.roll(g, LANES // 2, 1) * s]
        acc = jnp.concatenate(parts, axis=1)
    if has_res:
        acc = acc + res_ref[...]
    o_ref[...] = acc.astype(o_ref.dtype)


def _proj_tiles(T, K, N, a_bytes, out_bytes, has_norm, has_res, side_bytes, min_steps):
    for tm in (1024, 512, 256):
        if T % tm:
            continue
        for tn in (1024, 512, 256, LANES):
            if N % tn or (T // tm) * (N // tn) < min_steps:
                continue
            need = (2 * tm * K * a_bytes + (tm * K * 2 if has_norm else 0) + 2 * K * tn * 2 + side_bytes
                    + 2 * tm * tn * out_bytes + (2 * tm * tn * 4 if has_res else 0) + tm * tn * 4)
            if need <= PROJ_VMEM_BUDGET_BYTES:
                return tm, tn
    raise ValueError("no projection tiling fits VMEM")


def _proj(a, w, *, cols=None, norm_w=None, res=None, epi=None, cos=None, sin=None, out_dtype=F32, casts=(),
          name="proj"):
    T, K = a.shape
    c0, N = (0, w.shape[1]) if cols is None else cols
    has_norm = norm_w is not None
    has_res = res is not None
    side_bytes = sum(2 * br * bc * (4 + 2) for _, _, (br, bc) in casts)
    min_steps = max([(s.shape[1] // br) * (s.shape[2] // bc) for s, _, (br, bc) in casts], default=1)
    tm, tn = _proj_tiles(T, K, N, a.dtype.itemsize, jnp.dtype(out_dtype).itemsize, has_norm, has_res, side_bytes,
                         min_steps)
    assert c0 % tn == 0
    nb0 = c0 // tn
    in_specs = [pl.BlockSpec((tm, K), lambda m, n: (m, 0))]
    args = [a]
    if has_norm:
        in_specs.append(pl.BlockSpec((1, K), lambda m, n: (0, 0)))
        args.append(norm_w.reshape(1, K).astype(F32))
    in_specs.append(pl.BlockSpec((K, tn), lambda m, n: (0, nb0 + n)))
    args.append(w)
    if has_res:
        in_specs.append(pl.BlockSpec((tm, tn), lambda m, n: (m, n)))
        args.append(res)
    if epi is not None:
        in_specs += [pl.BlockSpec((tm, LANES), lambda m, n: (m, 0))] * 2
        args += [cos, sin]
    scratch = [pltpu.VMEM((tm, K), BF16)] if has_norm else []
    out_specs = [pl.BlockSpec((tm, tn), lambda m, n: (m, n))]
    out_shape = [jax.ShapeDtypeStruct((T, N), out_dtype)]
    n_n = N // tn
    n_steps = (T // tm) * n_n
    for src, layer, (br, bc) in casts:
        _, R, C = src.shape
        assert R % br == 0 and C % bc == 0 and (R // br) * (C // bc) <= n_steps
        ncc, last = C // bc, (R // br) * (C // bc) - 1

        def chunk(m, n, ncc=ncc, last=last):
            c = jnp.minimum(m * n_n + n, last)
            return c // ncc, c % ncc

        in_specs.append(pl.BlockSpec((None, br, bc), lambda m, n, layer=layer, chunk=chunk: (layer, *chunk(m, n))))
        args.append(src)
        out_specs.append(pl.BlockSpec((br, bc), chunk))
        out_shape.append(jax.ShapeDtypeStruct((R, C), BF16))
    outs = pl.pallas_call(
        functools.partial(_proj_body, has_norm=has_norm, epi=epi, has_res=has_res, n_casts=len(casts)),
        grid=(T // tm, n_n),
        in_specs=in_specs,
        out_specs=out_specs,
        out_shape=out_shape,
        scratch_shapes=scratch,
        compiler_params=_cparams("arbitrary", "arbitrary"),
        name=name,
    )(*args)
    return outs if casts else outs[0]


def _rmsnorm_body(x_ref, w_ref, o_ref):
    o_ref[...] = _rms_rows(x_ref[...], w_ref[...])


def _rmsnorm(x, w, tm=512):
    T, D = x.shape
    return pl.pallas_call(
        _rmsnorm_body,
        grid=(T // tm,),
        in_specs=[pl.BlockSpec((tm, D), lambda m: (m, 0)), pl.BlockSpec((1, D), lambda m: (0, 0))],
        out_specs=pl.BlockSpec((tm, D), lambda m: (m, 0)),
        out_shape=jax.ShapeDtypeStruct((T, D), F32),
        compiler_params=_cparams("parallel"),
        name="final_norm",
    )(x, w.reshape(1, D).astype(F32))


def _ffn_body(x_ref, nw_ref, wg_ref, wu_ref, cg_ref, cu_ref, cgp_ref, cup_ref, wd_ref, xres_ref, o_ref,
              h_ref, act_ref, ag_ref, au_ref, bg_ref, bu_ref, carry_g_ref, carry_u_ref, *, tiles_per_seq, nf):
    m = pl.program_id(0)
    f = pl.program_id(1)
    tm = x_ref.shape[0]
    tf = wg_ref.shape[1]
    half = tm // 2

    def activation(g_ref, u_ref, cwg, cwu):
        def conv(buf_ref, cw):
            win = buf_ref[...]
            return (cw[0:1, :] * pltpu.roll(win, 2, 0)[SUBLANES:, :]
                    + cw[1:2, :] * pltpu.roll(win, 1, 0)[SUBLANES:, :]
                    + cw[2:3, :] * win[SUBLANES:, :] + cw[3:4, :])
        return (_silu(conv(g_ref, cwg)) * conv(u_ref, cwu)).astype(BF16)

    def act_cols(fi):
        return pl.ds(pl.multiple_of(fi * tf, tf), tf)

    @pl.when(f == 0)
    def _():
        h_ref[...] = _rms_rows(x_ref[...], nw_ref[...]).astype(BF16)

    @pl.when(jnp.logical_and(f == 0, m == 0))
    def _():
        carry_g_ref[...] = jnp.zeros_like(carry_g_ref)
        carry_u_ref[...] = jnp.zeros_like(carry_u_ref)
        bg_ref[...] = jnp.zeros_like(bg_ref)
        bu_ref[...] = jnp.zeros_like(bu_ref)

    def finish_previous(fi):
        act_ref[0:half, act_cols(fi)] = activation(ag_ref, au_ref, cgp_ref[...], cup_ref[...])
        act_ref[half:, act_cols(fi)] = activation(bg_ref, bu_ref, cgp_ref[...], cup_ref[...])

    @pl.when(f < nf)
    def _up():
        finish_previous(jnp.maximum(f - 1, 0))
        seq_start = m % tiles_per_seq == 0
        ag_ref[0:SUBLANES, :] = jnp.where(seq_start, 0.0, carry_g_ref[f])
        au_ref[0:SUBLANES, :] = jnp.where(seq_start, 0.0, carry_u_ref[f])
        h_a = h_ref[0:half, :]
        ag_ref[SUBLANES:, :] = _dot(h_a, wg_ref[...])
        au_ref[SUBLANES:, :] = _dot(h_a, wu_ref[...])
        h_b = h_ref[half:, :]
        bg_ref[0:SUBLANES, :] = ag_ref[half:half + SUBLANES, :]
        bu_ref[0:SUBLANES, :] = au_ref[half:half + SUBLANES, :]
        bg_ref[SUBLANES:, :] = _dot(h_b, wg_ref[...])
        bu_ref[SUBLANES:, :] = _dot(h_b, wu_ref[...])
        carry_g_ref[f] = bg_ref[half:half + SUBLANES, :]
        carry_u_ref[f] = bu_ref[half:half + SUBLANES, :]

    @pl.when(f == nf)
    def _():
        finish_previous(nf - 1)

    @pl.when(f >= nf)
    def _down():
        o_ref[...] = xres_ref[...] + _dot(act_ref[...], wd_ref[...])


def _ffn(x, norm_w, w_up, cw, w_down, layer, seq_len, tm=1024, tf=512, td=256):
    T, D = x.shape
    F = w_down.shape[0]
    assert T % tm == 0 and F % tf == 0 and seq_len % tm == 0 and D % td == 0 and (tm // 2) % SUBLANES == 0
    nf = F // tf
    nd = D // td
    cur = lambda f: jnp.minimum(f, nf - 1)
    prev = lambda f: jnp.maximum(jnp.minimum(f, nf) - 1, 0)
    gate_blk = lambda m, f: (0, cur(f))
    up_blk = lambda m, f: (0, nf + cur(f))
    cgate_blk = lambda m, f: (layer, 0, cur(f))
    cup_blk = lambda m, f: (layer, 0, nf + cur(f))
    gate_prev = lambda m, f: (layer, 0, prev(f))
    up_prev = lambda m, f: (layer, 0, nf + prev(f))
    down_blk = lambda m, f: (0, jnp.maximum(f - nf, 0))
    out_blk = lambda m, f: (m, jnp.maximum(f - nf, 0))
    stage = pltpu.VMEM((SUBLANES + tm // 2, tf), F32)
    return pl.pallas_call(
        functools.partial(_ffn_body, tiles_per_seq=seq_len // tm, nf=nf),
        grid=(T // tm, nf + nd),
        in_specs=[
            pl.BlockSpec((tm, D), lambda m, f: (m, 0)),
            pl.BlockSpec((1, D), lambda m, f: (0, 0)),
            pl.BlockSpec((D, tf), gate_blk),
            pl.BlockSpec((D, tf), up_blk),
            pl.BlockSpec((None, SUBLANES, tf), cgate_blk),
            pl.BlockSpec((None, SUBLANES, tf), cup_blk),
            pl.BlockSpec((None, SUBLANES, tf), gate_prev),
            pl.BlockSpec((None, SUBLANES, tf), up_prev),
            pl.BlockSpec((F, td), down_blk),
            pl.BlockSpec((tm, td), out_blk),
        ],
        out_specs=pl.BlockSpec((tm, td), out_blk),
        out_shape=jax.ShapeDtypeStruct((T, D), F32),
        scratch_shapes=[
            pltpu.VMEM((tm, D), BF16),
            pltpu.VMEM((tm, F), BF16),
            stage, stage, stage, stage,
            pltpu.VMEM((nf, SUBLANES, tf), F32),
            pltpu.VMEM((nf, SUBLANES, tf), F32),
        ],
        compiler_params=_cparams("arbitrary", "arbitrary"),
        name="conv_ffn",
    )(x, norm_w.reshape(1, D).astype(F32), w_up, w_up, cw, cw, cw, cw, w_down, x)


def _cast_spec(src, layer, axis, max_chunks):
    shape = src.shape[1:]
    for width in (LANES, 2 * LANES, 4 * LANES, 8 * LANES):
        if shape[axis] % width == 0 and shape[axis] // width <= max_chunks:
            return src, layer, ((width, shape[1]) if axis == 0 else (shape[0], width))
    raise ValueError("no chunking of the cast source fits the host's grid")


def _conv_rows(w, b):
    K, C = w.shape[-2:]
    pad = jnp.zeros(w.shape[:-2] + (SUBLANES - K - 1, C), F32)
    return jnp.concatenate([w.astype(F32), b.astype(F32)[..., None, :], pad], axis=-2)


def _mla_down_body(x_ref, nw_ref, w_ref, qn_ref, kvn_ref, cos_ref, sin_ref, cq_ref, ckv_ref, kpe_ref):
    h = _rms_rows(x_ref[...], nw_ref[...]).astype(BF16)
    r = _dot(h, w_ref[...])
    cq_ref[...] = _rms_rows(r[:, :MLA_Q_RANK], qn_ref[...]).astype(BF16)
    ckv_ref[...] = _rms_rows(r[:, MLA_Q_RANK:MLA_Q_RANK + MLA_KV_RANK], kvn_ref[...]).astype(BF16)
    g = r[:, MLA_Q_RANK + MLA_KV_RANK:]
    kpe_ref[...] = (g * cos_ref[...] + pltpu.roll(g, LANES // 2, 1) * sin_ref[...]).astype(BF16)


def _mla_down(x, norm_w, w, q_norm, kv_norm, cos, sin, tm=512):
    T, D = x.shape
    N = w.shape[1]
    row = lambda n: pl.BlockSpec((1, n), lambda m: (0, 0))
    tile = lambda n: pl.BlockSpec((tm, n), lambda m: (m, 0))
    return pl.pallas_call(
        _mla_down_body,
        grid=(T // tm,),
        in_specs=[tile(D), row(D), pl.BlockSpec((D, N), lambda m: (0, 0)), row(MLA_Q_RANK), row(MLA_KV_RANK),
                  tile(LANES), tile(LANES)],
        out_specs=[tile(MLA_Q_RANK), tile(MLA_KV_RANK), tile(LANES)],
        out_shape=[jax.ShapeDtypeStruct((T, MLA_Q_RANK), BF16), jax.ShapeDtypeStruct((T, MLA_KV_RANK), BF16),
                   jax.ShapeDtypeStruct((T, LANES), BF16)],
        compiler_params=_cparams("parallel"),
        name="mla_down",
    )(x, norm_w.reshape(1, D).astype(F32), w, q_norm.reshape(1, -1).astype(F32),
      kv_norm.reshape(1, -1).astype(F32), cos, sin)


def _lane_tile(x, width):
    return jnp.concatenate([x] * (width // LANES), axis=1)


def _mla_attn_body(q_ref, kn_ref, kpe_ref, v_ref, o_ref, kcat_ref, m_ref, l_ref, acc_ref, *, tk, hp):
    i = pl.program_id(2)
    tq = q_ref.shape[0]

    @pl.when(i == 0)
    def _():
        for h in range(hp):
            kcat_ref[h, :, :LANES] = kn_ref[:, h * LANES:(h + 1) * LANES]
            kcat_ref[h, :, LANES:] = kpe_ref[...]

    m_ref[...] = jnp.full(m_ref.shape, -jnp.inf, F32)
    l_ref[...] = jnp.zeros_like(l_ref)
    acc_ref[...] = jnp.zeros_like(acc_ref)
    qs = [q_ref[:, h * 2 * LANES:(h + 1) * 2 * LANES] for h in range(hp)]

    def step(j, masked):
        ks = pl.ds(pl.multiple_of(j * tk, tk), tk)
        ss = [_dot_nt(qs[h], kcat_ref[h, ks, :]) for h in range(hp)]
        for h in range(hp):
            s = ss[h]
            if masked:
                row = lax.broadcasted_iota(jnp.int32, (tq, tk), 0)
                col = lax.broadcasted_iota(jnp.int32, (tq, tk), 1)
                s = jnp.where(col <= row, s, -jnp.inf)
            m_prev = m_ref[h]
            m_new = jnp.maximum(m_prev, jnp.max(s, axis=-1, keepdims=True))
            alpha = jnp.exp2(m_prev - m_new)
            p = jnp.exp2(s - _lane_tile(m_new, tk))
            l_ref[h] = alpha * l_ref[h] + jnp.sum(p, axis=-1, keepdims=True)
            acc_ref[h] = alpha * acc_ref[h] + _dot(p.astype(BF16), v_ref[ks, h * LANES:(h + 1) * LANES])
            m_ref[h] = m_new

    def body(j, c):
        step(j, False)
        return c

    lax.fori_loop(0, i, body, 0)
    step(i, True)
    for h in range(hp):
        o_ref[:, h * LANES:(h + 1) * LANES] = (acc_ref[h] / l_ref[h]).astype(o_ref.dtype)


def _mla_attn(q, kv, kpe, batch, seq_len, tq=512, hp=4):
    T = q.shape[0]
    H = MLA_HEADS
    assert MLA_V == LANES and MLA_NOPE == LANES and H % hp == 0
    nq = seq_len // tq
    ng = H // hp
    return pl.pallas_call(
        functools.partial(_mla_attn_body, tk=tq, hp=hp),
        grid=(batch, ng, nq),
        in_specs=[
            pl.BlockSpec((tq, hp * 2 * LANES), lambda b, g, i: (b * nq + i, g)),
            pl.BlockSpec((seq_len, hp * LANES), lambda b, g, i: (b, g)),
            pl.BlockSpec((seq_len, LANES), lambda b, g, i: (b, 0)),
            pl.BlockSpec((seq_len, hp * LANES), lambda b, g, i: (b, ng + g)),
        ],
        out_specs=pl.BlockSpec((tq, hp * LANES), lambda b, g, i: (b * nq + i, g)),
        out_shape=jax.ShapeDtypeStruct((T, H * MLA_V), BF16),
        scratch_shapes=[
            pltpu.VMEM((hp, seq_len, 2 * LANES), BF16),
            pltpu.VMEM((hp, tq, LANES), F32),
            pltpu.VMEM((hp, tq, LANES), F32),
            pltpu.VMEM((hp, tq, LANES), F32),
        ],
        compiler_params=_cparams("parallel", "parallel", "arbitrary"),
        name="mla_attn",
    )(q, kv, kpe, kv)


def _mla_layer(x, norm_w, tabs, wq_a, q_norm, wq_b, wkv_a, kv_norm, wkv_b, wo, batch, seq_len,
               ffn_w_up, ffn_w_down, layer):
    cos_a, sin_a = tabs[0], tabs[1]
    D = x.shape[1]
    H = MLA_HEADS
    half = MLA_ROPE // 2
    scale = (MLA_NOPE + MLA_ROPE) ** -0.5 * LOG2_E
    zk = jnp.zeros((D, half), F32)
    w_down = jnp.concatenate(
        [wq_a, wkv_a[:, :MLA_KV_RANK], wkv_a[:, MLA_KV_RANK:MLA_KV_RANK + half], zk,
         wkv_a[:, MLA_KV_RANK + half:], zk], axis=1).astype(BF16)
    wq = wq_b.reshape(MLA_Q_RANK, H, MLA_NOPE + MLA_ROPE)
    zq = jnp.zeros((MLA_Q_RANK, H, half), F32)
    wq = jnp.concatenate([wq[:, :, :MLA_NOPE], wq[:, :, MLA_NOPE:MLA_NOPE + half], zq,
                          wq[:, :, MLA_NOPE + half:], zq], axis=2)
    wq = (wq * scale).reshape(MLA_Q_RANK, H * 2 * LANES).astype(BF16)
    wkv = wkv_b.reshape(MLA_KV_RANK, H, 2, MLA_NOPE).transpose(0, 2, 1, 3).reshape(MLA_KV_RANK, 2 * H * MLA_NOPE)
    wkv = wkv.astype(BF16)

    cq, ckv, kpe = _mla_down(x, norm_w, w_down, q_norm, kv_norm, cos_a, sin_a)
    q, ffn_up = _proj(cq, wq, epi="rope_group", cos=cos_a, sin=sin_a, out_dtype=BF16,
                      casts=(_cast_spec(ffn_w_up, layer, 1, 32),), name="mla_q")
    kv, ffn_down = _proj(ckv, wkv, out_dtype=BF16, casts=(_cast_spec(ffn_w_down, layer, 0, 32),), name="mla_kv")
    o = _mla_attn(q, kv, kpe, batch, seq_len)
    return _proj(o, wo.astype(BF16), res=x, name="mla_out"), ffn_up, ffn_down


def _sb_attn_body(q_ref, k_ref, v_ref, after_ref, o_ref, acc_ref, drop_ref, *, tk, sub, hp):
    i = pl.program_id(2)
    tq = q_ref.shape[0]
    acc_ref[...] = jnp.zeros_like(acc_ref)
    drop_ref[...] = jnp.zeros_like(drop_ref)
    qs = [q_ref[:, h * LANES:(h + 1) * LANES] for h in range(hp)]
    after = after_ref[...]

    def step(j, masked):
        ks = pl.ds(pl.multiple_of(j * tk, tk), tk)
        zs = [_dot_nt(qs[h], k_ref[ks, h * LANES:(h + 1) * LANES]) for h in range(hp)]
        for h in range(hp):
            z = zs[h]
            sp = jnp.maximum(z, 0.0) + jnp.log2(1.0 + jnp.exp2(_neg_abs(z)))
            if masked:
                row = lax.broadcasted_iota(jnp.int32, (tq, tk), 0)
                col = lax.broadcasted_iota(jnp.int32, (tq, tk), 1)
                strict = col < row
                sp = jnp.where(strict, sp, 0.0)
            drop = drop_ref[h]
            parts = [None] * (tk // sub)
            for b in reversed(range(tk // sub)):
                cs = slice(b * sub, (b + 1) * sub)
                sp_b = sp[:, cs]
                within = _dot(sp_b.astype(BF16), after)
                a_b = jnp.exp2(z[:, cs] - sp_b - within - _lane_tile(drop, sub))
                if masked:
                    a_b = jnp.where(strict[:, cs], a_b, 0.0)
                parts[b] = a_b.astype(BF16)
                drop = drop + jnp.sum(sp_b, axis=-1, keepdims=True)
            acc_ref[h] += _dot(jnp.concatenate(parts, axis=1), v_ref[ks, h * LANES:(h + 1) * LANES])
            drop_ref[h] = drop

    step(i, True)

    def live():
        return jnp.min(drop_ref[...]) < SB_DEAD_LOG2

    def body(carry):
        t, _ = carry
        step(i - 1 - t, False)
        return t + 1, live()

    lax.while_loop(lambda c: jnp.logical_and(c[0] < i, c[1]), body, (jnp.int32(0), live()))
    for h in range(hp):
        o_ref[:, h * LANES:(h + 1) * LANES] = acc_ref[h].astype(o_ref.dtype)


def _sb_attn(qkv, batch, seq_len, tq=512, sub=256, hp=2):
    T = qkv.shape[0]
    H, Dh = SB_HEADS, SB_HEAD_DIM
    assert Dh == LANES and H % hp == 0
    nq = seq_len // tq
    ng = H // hp
    after = (jnp.arange(sub)[:, None] > jnp.arange(sub)[None, :]).astype(BF16)
    return pl.pallas_call(
        functools.partial(_sb_attn_body, tk=tq, sub=sub, hp=hp),
        grid=(batch, ng, nq),
        in_specs=[
            pl.BlockSpec((tq, hp * Dh), lambda b, g, i: (b * nq + i, g)),
            pl.BlockSpec((seq_len, hp * Dh), lambda b, g, i: (b, ng + g)),
            pl.BlockSpec((seq_len, hp * Dh), lambda b, g, i: (b, 2 * ng + g)),
            pl.BlockSpec((sub, sub), lambda b, g, i: (0, 0)),
        ],
        out_specs=pl.BlockSpec((tq, hp * Dh), lambda b, g, i: (b * nq + i, g)),
        out_shape=jax.ShapeDtypeStruct((T, H * Dh), BF16),
        scratch_shapes=[pltpu.VMEM((hp, tq, Dh), F32), pltpu.VMEM((hp, tq, LANES), F32)],
        compiler_params=_cparams("parallel", "parallel", "arbitrary"),
        name="sb_attn",
    )(qkv, qkv, qkv, after)


def _sb_layer(x, norm_w, wqkv, wo, batch, seq_len, ffn_w_up, ffn_w_down, layer):
    n_q = SB_HEADS * SB_HEAD_DIM
    q_scale = SB_HEAD_DIM ** -0.5 * LOG2_E
    col_scale = jnp.where(jnp.arange(wqkv.shape[1]) < n_q, q_scale, 1.0).astype(F32)
    w = (wqkv * col_scale[None, :]).astype(BF16)
    qkv, ffn_up, ffn_down = _proj(
        x, w, norm_w=norm_w, out_dtype=BF16, name="sb_qkv",
        casts=(_cast_spec(ffn_w_up, layer, 1, 48), _cast_spec(ffn_w_down, layer, 0, 48)))
    o = _sb_attn(qkv, batch, seq_len)
    return _proj(o, wo.astype(BF16), res=x, name="sb_out"), ffn_up, ffn_down


def _ssd_body(zx_ref, dt_ref, cw_ref, dtb_ref, alog_ref, dfull_ref, nw_ref, exp_ref, o_ref,
              state_ref, carry_ref, buf_ref, y_ref, *, d_inner):
    c = pl.program_id(1)
    L = zx_ref.shape[0]
    G, N, P = SSD_GROUPS, SSD_STATE, SSD_HEAD_DIM
    gw = d_inner // G
    conv_ch = d_inner + 2 * G * N

    @pl.when(c == 0)
    def _():
        state_ref[...] = jnp.zeros_like(state_ref)
        carry_ref[...] = jnp.zeros_like(carry_ref)

    u = zx_ref[:, d_inner:d_inner + conv_ch]
    buf_ref[0:SUBLANES, :] = carry_ref[...]
    buf_ref[SUBLANES:SUBLANES + L, :] = u
    carry_ref[...] = u[L - SUBLANES:L, :]
    cw = cw_ref[...]
    conv = cw[SSD_CONV:SSD_CONV + 1, :] + cw[SSD_CONV - 1:SSD_CONV, :] * u
    win = buf_ref[...]
    for k in range(SSD_CONV - 1):
        conv = conv + cw[k:k + 1, :] * pltpu.roll(win, SSD_CONV - 1 - k, 0)[SUBLANES:, :]
    xbc = _silu(conv)
    xs = xbc[:, :d_inner]
    bm = xbc[:, d_inner:d_inner + G * N]
    cm = xbc[:, d_inner + G * N:]

    dt = _softplus(dt_ref[...] + dtb_ref[...])
    a = dt * (-jnp.exp(alog_ref[...]))
    r_i = lax.broadcasted_iota(jnp.int32, (L, L), 0)
    c_i = lax.broadcasted_iota(jnp.int32, (L, L), 1)
    causal = c_i <= r_i
    acum_t = _dot_split(a.T, jnp.where(c_i >= r_i, 1.0, 0.0).astype(BF16))
    acum = acum_t.T
    expand = exp_ref[...]
    dt_full = _dot_split(dt, expand)
    acum_full = _dot_split(acum, expand)
    last_full = acum_full[L - 1:L, :]
    xdt = xs * dt_full
    xdt_bf = xdt.astype(BF16)
    xdec_bf = (xdt * jnp.exp(last_full - acum_full)).astype(BF16)
    grow = jnp.exp(acum_full)
    chunk_decay = jnp.exp(last_full)
    lane = lax.broadcasted_iota(jnp.int32, (L, LANES), 1)
    first_half = lane < P

    for g in range(G):
        b_g = bm[:, g * N:(g + 1) * N]
        c_g = cm[:, g * N:(g + 1) * N].astype(BF16)
        cb = _dot_nt(c_g, b_g.astype(BF16))
        prev = state_ref[g]
        gs = slice(g * gw, (g + 1) * gw)
        y_off = _dot(c_g, prev.astype(BF16)) * grow[:, gs]
        parts = []
        for pair in range(gw // LANES):
            h0 = g * (gw // P) + 2 * pair
            xp = xdt_bf[:, h0 * P:h0 * P + LANES]
            ms = []
            for h in (h0, h0 + 1):
                seg = acum[:, h:h + 1] - acum_t[h:h + 1, :]
                ms.append((cb * jnp.exp(jnp.where(causal, seg, -jnp.inf))).astype(BF16))
            zero = jnp.zeros_like(xp)
            parts.append(_dot(ms[0], jnp.where(first_half, xp, zero)) + _dot(ms[1], jnp.where(first_half, zero, xp)))
        y_ref[:, gs] = jnp.concatenate(parts, axis=1) + y_off
        state_ref[g] = prev * chunk_decay[:, gs] + _dot(b_g.T.astype(BF16), xdec_bf[:, gs])

    y = (y_ref[...] + dfull_ref[...] * xs) * _silu(zx_ref[:, :d_inner])
    nw = nw_ref[...]
    for g in range(G):
        gs = slice(g * gw, (g + 1) * gw)
        o_ref[:, gs] = _rms_rows(y[:, gs], nw[:, gs]).astype(o_ref.dtype)


def _ssd_core(zx, dt_raw, cw, dt_bias, a_log, d_full, norm_w, expand, batch, seq_len, chunk=128):
    T = zx.shape[0]
    d_inner = d_full.shape[1]
    conv_ch = cw.shape[1]
    nc = seq_len // chunk
    G, N = SSD_GROUPS, SSD_STATE
    row = lambda n: pl.BlockSpec((1, n), lambda b, c: (0, 0))
    return pl.pallas_call(
        functools.partial(_ssd_body, d_inner=d_inner),
        grid=(batch, nc),
        in_specs=[
            pl.BlockSpec((chunk, zx.shape[1]), lambda b, c: (b * nc + c, 0)),
            pl.BlockSpec((chunk, LANES), lambda b, c: (b * nc + c, 0)),
            pl.BlockSpec((SUBLANES, conv_ch), lambda b, c: (0, 0)),
            row(LANES), row(LANES), row(d_inner), row(d_inner),
            pl.BlockSpec((LANES, d_inner), lambda b, c: (0, 0)),
        ],
        out_specs=pl.BlockSpec((chunk, d_inner), lambda b, c: (b * nc + c, 0)),
        out_shape=jax.ShapeDtypeStruct((T, d_inner), BF16),
        scratch_shapes=[
            pltpu.VMEM((G, N, d_inner // G), F32),
            pltpu.VMEM((SUBLANES, conv_ch), F32),
            pltpu.VMEM((SUBLANES + chunk, conv_ch), F32),
            pltpu.VMEM((chunk, d_inner), F32),
        ],
        compiler_params=_cparams("parallel", "arbitrary"),
        name="ssd_core",
    )(zx, dt_raw, cw, dt_bias, a_log, d_full, norm_w, expand)


def _ssd_layer(x, norm_w, w_in, conv_w, conv_b, dt_bias, a_log, d_skip, ssd_norm, w_out, batch, seq_len,
               ffn_w_up, ffn_w_down, layer):
    heads = d_skip.shape[0]
    d_inner = heads * SSD_HEAD_DIM
    n_zx = w_in.shape[1] - heads
    assert heads <= LANES
    pad = lambda v: jnp.concatenate([v.astype(F32), jnp.zeros((LANES - heads,), F32)]).reshape(1, LANES)
    w_dt = jnp.concatenate([w_in[:, n_zx:], jnp.zeros((w_in.shape[0], LANES - heads), F32)], axis=1).astype(BF16)
    zx, ffn_up = _proj(x, w_in.astype(BF16), cols=(0, n_zx), norm_w=norm_w, name="ssd_in",
                       casts=(_cast_spec(ffn_w_up, layer, 1, 64),))
    dt_raw = _proj(x, w_dt, norm_w=norm_w, name="ssd_dt")
    expand = (jnp.arange(LANES)[:, None] == (jnp.arange(d_inner) // SSD_HEAD_DIM)[None, :]).astype(BF16)
    d_full = jnp.repeat(d_skip.astype(F32), SSD_HEAD_DIM).reshape(1, d_inner)
    y = _ssd_core(zx, dt_raw, _conv_rows(conv_w, conv_b), pad(dt_bias), pad(a_log), d_full,
                  ssd_norm.reshape(1, d_inner).astype(F32), expand, batch, seq_len)
    x, ffn_down = _proj(y, w_out.astype(BF16), res=x, name="ssd_out", casts=(_cast_spec(ffn_w_down, layer, 0, 32),))
    return x, ffn_up, ffn_down


def _ret_body(qk_ref, v_ref, g_ref, o_ref, state_ref, *, heads, dk, dv):
    c = pl.program_id(1)
    L = qk_ref.shape[0]

    @pl.when(c == 0)
    def _():
        state_ref[...] = jnp.zeros_like(state_ref)

    r_i = lax.broadcasted_iota(jnp.int32, (L, L), 0)
    c_i = lax.broadcasted_iota(jnp.int32, (L, L), 1)
    diff = (r_i - c_i).astype(F32)
    lower = diff >= 0
    idx = lax.broadcasted_iota(jnp.int32, (L, 1), 0).astype(F32)

    for h in range(heads):
        log_g = math.log(1.0 - 2.0 ** (-5.0 - h))
        d_intra = jnp.where(lower, jnp.exp(jnp.maximum(diff, 0.0) * log_g), 0.0)
        k_dec = jnp.exp((L - 1.0 - idx) * log_g)
        q_dec = jnp.exp((idx + 1.0) * log_g)
        q = qk_ref[:, h * dk:(h + 1) * dk]
        k = qk_ref[:, (heads + h) * dk:(heads + h + 1) * dk]
        vs = slice(h * dv, (h + 1) * dv)
        v = v_ref[:, vs]
        prev = state_ref[h]
        scores = (_dot_nt(q, k) * d_intra).astype(BF16)
        o = _dot(scores, v) + q_dec * _dot(q, prev.astype(BF16))
        kd_t = (k.astype(F32) * k_dec).T.astype(BF16)
        state_ref[h] = math.exp(L * log_g) * prev + _dot(kd_t, v)
        ms = jnp.mean(o * o, axis=-1, keepdims=True)
        o_ref[:, vs] = (_silu(g_ref[:, vs]) * (o * lax.rsqrt(ms + RMS_EPS))).astype(o_ref.dtype)


def _ret_core(qk, v, g, batch, seq_len, dk, dv, chunk=256):
    T = qk.shape[0]
    H = RET_HEADS
    nc = seq_len // chunk
    blk = lambda n: pl.BlockSpec((chunk, n), lambda b, c: (b * nc + c, 0))
    return pl.pallas_call(
        functools.partial(_ret_body, heads=H, dk=dk, dv=dv),
        grid=(batch, nc),
        in_specs=[blk(2 * H * dk), blk(H * dv), blk(H * dv)],
        out_specs=blk(H * dv),
        out_shape=jax.ShapeDtypeStruct((T, H * dv), BF16),
        scratch_shapes=[pltpu.VMEM((H, dk, dv), F32)],
        compiler_params=_cparams("parallel", "arbitrary"),
        name="ret_core",
    )(qk, v, g)


def _ret_layer(x, norm_w, tabs, w_in, wo, batch, seq_len, ffn_w_up, ffn_w_down, layer):
    cos_r, sin_r = tabs[2], tabs[3]
    D = x.shape[1]
    H = RET_HEADS
    dk = D // H
    dv = 2 * D // H
    n_qk = 2 * H * dk
    n_v = H * dv
    n_all = w_in.shape[1]
    col = jnp.arange(n_all)
    col_scale = jnp.where((col >= H * dk) & (col < n_qk), dk ** -0.5, 1.0).astype(F32)
    w = (w_in * col_scale[None, :]).astype(BF16)
    qk, ffn_up = _proj(x, w, cols=(0, n_qk), norm_w=norm_w, epi="rope_half", cos=cos_r, sin=sin_r, out_dtype=BF16,
                       casts=(_cast_spec(ffn_w_up, layer, 1, 32),), name="ret_qk")
    v, ffn_down = _proj(x, w, cols=(n_qk, n_v), norm_w=norm_w, out_dtype=BF16,
                        casts=(_cast_spec(ffn_w_down, layer, 0, 32),), name="ret_v")
    g = _proj(x, w, cols=(n_qk + n_v, n_all - n_qk - n_v), norm_w=norm_w, name="ret_g")
    o = _ret_core(qk, v, g, batch, seq_len, dk, dv)
    return _proj(o, wo.astype(BF16), res=x, name="ret_out"), ffn_up, ffn_down


def kernel(x, positions, norm_mix, norm_ffn, norm_final, mla_wq_a, mla_q_norm, mla_wq_b, mla_wkv_a, mla_kv_norm, mla_wkv_b, mla_wo, sb_wqkv, sb_wo, ssd_w_in, ssd_conv_w, ssd_conv_b, ssd_dt_bias, ssd_a_log, ssd_d, ssd_norm, ssd_w_out, ret_w_in, ret_wo, ffn_w_up, ffn_conv_w, ffn_conv_b, ffn_w_down):
    B, S, D = x.shape
    depth = norm_mix.shape[0]
    n_mixers = 4
    tabs = _rope_tables(positions, D // RET_HEADS)
    cw = _conv_rows(ffn_conv_w, ffn_conv_b)
    xt = x.reshape(B * S, D)
    for i in range(depth):
        m, j = i % n_mixers, i // n_mixers
        ffn = (ffn_w_up, ffn_w_down, i)
        if m == 0:
            xt, w_up, w_down = _mla_layer(xt, norm_mix[i], tabs, mla_wq_a[j], mla_q_norm[j], mla_wq_b[j],
                                          mla_wkv_a[j], mla_kv_norm[j], mla_wkv_b[j], mla_wo[j], B, S, *ffn)
        elif m == 1:
            xt, w_up, w_down = _sb_layer(xt, norm_mix[i], sb_wqkv[j], sb_wo[j], B, S, *ffn)
        elif m == 2:
            xt, w_up, w_down = _ssd_layer(xt, norm_mix[i], ssd_w_in[j], ssd_conv_w[j], ssd_conv_b[j],
                                          ssd_dt_bias[j], ssd_a_log[j], ssd_d[j], ssd_norm[j], ssd_w_out[j],
                                          B, S, *ffn)
        else:
            xt, w_up, w_down = _ret_layer(xt, norm_mix[i], tabs, ret_w_in[j], ret_wo[j], B, S, *ffn)
        xt = _ffn(xt, norm_ffn[i], w_up, cw, w_down, i, S)
    return _rmsnorm(xt, norm_final).reshape(B, S, D)
```

```python
import functools
import math

import jax
import jax.numpy as jnp
from jax import lax
from jax.experimental import pallas as pl
from jax.experimental.pallas import tpu as pltpu

F32 = jnp.float32
BF16 = jnp.bfloat16

RMS_EPS = 1e-6
ROPE_BASE = 10000.0
LOG2_E = 1.4426950408889634
SB_DEAD_LOG2 = 160.0

MLA_HEADS = 16
MLA_Q_RANK = 512
MLA_KV_RANK = 512
MLA_NOPE = 128
MLA_ROPE = 64
MLA_V = 128
SB_HEADS = 16
SB_HEAD_DIM = 128
SSD_HEAD_DIM = 64
SSD_GROUPS = 8
SSD_STATE = 128
SSD_CONV = 4
RET_HEADS = 8
FFN_CONV = 3

LANES = 128
SUBLANES = 8
VMEM_LIMIT_BYTES = 56 * 1024 * 1024
PROJ_VMEM_BUDGET_BYTES = 48 * 1024 * 1024


def _cparams(*sem):
    return pltpu.CompilerParams(dimension_semantics=sem, vmem_limit_bytes=VMEM_LIMIT_BYTES)


def _dot(a, b):
    return jnp.dot(a, b, preferred_element_type=F32)


def _dot_nt(a, b):
    return lax.dot_general(a, b, (((1,), (1,)), ((), ())), preferred_element_type=F32)


def _dot_split(a, b):
    hi = a.astype(BF16)
    lo = (a - hi.astype(F32)).astype(BF16)
    return _dot(hi, b) + _dot(lo, b)


def _rms_rows(xf, w):
    ms = jnp.mean(xf * xf, axis=-1, keepdims=True)
    return xf * lax.rsqrt(ms + RMS_EPS) * w


def _silu(x):
    return x * jax.nn.sigmoid(x)


def _neg_abs(x):
    bits = lax.bitcast_convert_type(x, jnp.uint32) | jnp.uint32(0x80000000)
    return lax.bitcast_convert_type(bits, F32)


def _softplus(x):
    return jnp.maximum(x, 0.0) + jnp.log(1.0 + jnp.exp(-jnp.abs(x)))


def _rope_tables_body(pos_ref, inv_a_ref, sgn_a_ref, inv_r_ref, cos_a_ref, sin_a_ref, cos_r_ref, sin_r_ref):
    pos = pos_ref[...]
    ang_a = pos * inv_a_ref[...]
    cos_a_ref[...] = jnp.cos(ang_a)
    sin_a_ref[...] = jnp.sin(ang_a) * sgn_a_ref[...]
    ang_r = pos * inv_r_ref[...]
    cos_r_ref[...] = jnp.cos(ang_r)
    sin_r_ref[...] = jnp.sin(ang_r)


def _rope_tables(positions, ret_dk):
    T = positions.size
    tm = 1024
    pos = positions.reshape(T, 1).astype(F32)
    half_a = MLA_ROPE // 2
    inv_a = 1.0 / (ROPE_BASE ** (jnp.arange(half_a, dtype=F32) * (2.0 / MLA_ROPE)))
    z = jnp.zeros((half_a,), F32)
    inv_a = jnp.concatenate([inv_a, z, inv_a, z]).reshape(1, LANES)
    o = jnp.ones((2 * half_a,), F32)
    sgn_a = jnp.concatenate([-o, o]).reshape(1, LANES)
    half_r = ret_dk // 2
    assert half_r == LANES
    inv_r = (1.0 / (ROPE_BASE ** (jnp.arange(half_r, dtype=F32) * (2.0 / ret_dk)))).reshape(1, LANES)
    row = pl.BlockSpec((1, LANES), lambda i: (0, 0))
    tab = pl.BlockSpec((tm, LANES), lambda i: (i, 0))
    shp = jax.ShapeDtypeStruct((T, LANES), F32)
    return pl.pallas_call(
        _rope_tables_body,
        grid=(T // tm,),
        in_specs=[pl.BlockSpec((tm, 1), lambda i: (i, 0)), row, row, row],
        out_specs=[tab, tab, tab, tab],
        out_shape=[shp, shp, shp, shp],
        compiler_params=_cparams("parallel"),
        name="rope_tables",
    )(pos, inv_a, sgn_a, inv_r)


def _proj_body(*refs, has_norm, epi, has_res, n_casts):
    it = iter(refs)
    a_ref = next(it)
    nw_ref = next(it) if has_norm else None
    w_ref = next(it)
    res_ref = next(it) if has_res else None
    cos_ref = sin_ref = None
    if epi is not None:
        cos_ref = next(it)
        sin_ref = next(it)
    cast_src = [next(it) for _ in range(n_casts)]
    o_ref = next(it)
    cast_dst = [next(it) for _ in range(n_casts)]
    abf_ref = next(it) if has_norm else None

    for src_ref, dst_ref in zip(cast_src, cast_dst):
        dst_ref[...] = src_ref[...].astype(BF16)

    if has_norm:
        @pl.when(pl.program_id(1) == 0)
        def _():
            abf_ref[...] = _rms_rows(a_ref[...].astype(F32), nw_ref[...]).astype(BF16)
        a = abf_ref[...]
    else:
        a = a_ref[...]
    acc = _dot(a, w_ref[...])
    if epi == "rope_half":
        c = cos_ref[...]
        s = sin_ref[...]
        parts = []
        for j in range(acc.shape[1] // (2 * LANES)):
            x1 = acc[:, (2 * j) * LANES:(2 * j + 1) * LANES]
            x2 = acc[:, (2 * j + 1) * LANES:(2 * j + 2) * LANES]
            parts += [x1 * c - x2 * s, x2 * c + x1 * s]
        acc = jnp.concatenate(parts, axis=1)
    elif epi == "rope_group":
        c = cos_ref[...]
        s = sin_ref[...]
        parts = []
        for j in range(acc.shape[1] // (2 * LANES)):
            g = acc[:, (2 * j + 1) * LANES:(2 * j + 2) * LANES]
            parts += [acc[:, (2 * j) * LANES:(2 * j + 1) * LANES], g * c + pltpu.roll(g, LANES // 2, 1) * s]
        acc = jnp.concatenate(parts, axis=1)
    if has_res:
        acc = acc + res_ref[...]
    o_ref[...] = acc.astype(o_ref.dtype)


def _proj_tiles(T, K, N, a_bytes, out_bytes, has_norm, has_res, side_bytes, min_steps):
    for tm in (1024, 512, 256):
        if T % tm:
            continue
        for tn in (1024, 512, 256, LANES):
            if N % tn or (T // tm) * (N // tn) < min_steps:
                continue
            need = (2 * tm * K * a_bytes + (tm * K * 2 if has_norm else 0) + 2 * K * tn * 2 + side_bytes
                    + 2 * tm * tn * out_bytes + (2 * tm * tn * 4 if has_res else 0) + tm * tn * 4)
            if need <= PROJ_VMEM_BUDGET_BYTES:
                return tm, tn
    raise ValueError("no projection tiling fits VMEM")


def _proj(a, w, *, cols=None, norm_w=None, res=None, epi=None, cos=None, sin=None, out_dtype=F32, casts=(),
          name="proj"):
    T, K = a.shape
    c0, N = (0, w.shape[1]) if cols is None else cols
    has_norm = norm_w is not None
    has_res = res is not None
    side_bytes = sum(2 * br * bc * (4 + 2) for _, _, (br, bc) in casts)
    min_steps = max([(s.shape[1] // br) * (s.shape[2] // bc) for s, _, (br, bc) in casts], default=1)
    tm, tn = _proj_tiles(T, K, N, a.dtype.itemsize, jnp.dtype(out_dtype).itemsize, has_norm, has_res, side_bytes,
                         min_steps)
    assert c0 % tn == 0
    nb0 = c0 // tn
    in_specs = [pl.BlockSpec((tm, K), lambda m, n: (m, 0))]
    args = [a]
    if has_norm:
        in_specs.append(pl.BlockSpec((1, K), lambda m, n: (0, 0)))
        args.append(norm_w.reshape(1, K).astype(F32))
    in_specs.append(pl.BlockSpec((K, tn), lambda m, n: (0, nb0 + n)))
    args.append(w)
    if has_res:
        in_specs.append(pl.BlockSpec((tm, tn), lambda m, n: (m, n)))
        args.append(res)
    if epi is not None:
        in_specs += [pl.BlockSpec((tm, LANES), lambda m, n: (m, 0))] * 2
        args += [cos, sin]
    scratch = [pltpu.VMEM((tm, K), BF16)] if has_norm else []
    out_specs = [pl.BlockSpec((tm, tn), lambda m, n: (m, n))]
    out_shape = [jax.ShapeDtypeStruct((T, N), out_dtype)]
    n_n = N // tn
    n_steps = (T // tm) * n_n
    for src, layer, (br, bc) in casts:
        _, R, C = src.shape
        assert R % br == 0 and C % bc == 0 and (R // br) * (C // bc) <= n_steps
        ncc, last = C // bc, (R // br) * (C // bc) - 1

        def chunk(m, n, ncc=ncc, last=last):
            c = jnp.minimum(m * n_n + n, last)
            return c // ncc, c % ncc

        in_specs.append(pl.BlockSpec((None, br, bc), lambda m, n, layer=layer, chunk=chunk: (layer, *chunk(m, n))))
        args.append(src)
        out_specs.append(pl.BlockSpec((br, bc), chunk))
        out_shape.append(jax.ShapeDtypeStruct((R, C), BF16))
    outs = pl.pallas_call(
        functools.partial(_proj_body, has_norm=has_norm, epi=epi, has_res=has_res, n_casts=len(casts)),
        grid=(T // tm, n_n),
        in_specs=in_specs,
        out_specs=out_specs,
        out_shape=out_shape,
        scratch_shapes=scratch,
        compiler_params=_cparams("arbitrary", "arbitrary"),
        name=name,
    )(*args)
    return outs if casts else outs[0]


def _rmsnorm_body(x_ref, w_ref, o_ref):
    o_ref[...] = _rms_rows(x_ref[...], w_ref[...])


def _rmsnorm(x, w, tm=512):
    T, D = x.shape
    return pl.pallas_call(
        _rmsnorm_body,
        grid=(T // tm,),
        in_specs=[pl.BlockSpec((tm, D), lambda m: (m, 0)), pl.BlockSpec((1, D), lambda m: (0, 0))],
        out_specs=pl.BlockSpec((tm, D), lambda m: (m, 0)),
        out_shape=jax.ShapeDtypeStruct((T, D), F32),
        compiler_params=_cparams("parallel"),
        name="final_norm",
    )(x, w.reshape(1, D).astype(F32))


def _ffn_body(x_ref, nw_ref, wg_ref, wu_ref, cg_ref, cu_ref, cgp_ref, cup_ref, wd_ref, xres_ref, o_ref,
              h_ref, act_ref, ag_ref, au_ref, bg_ref, bu_ref, carry_g_ref, carry_u_ref, *, tiles_per_seq, nf):
    m = pl.program_id(0)
    f = pl.program_id(1)
    tm = x_ref.shape[0]
    tf = wg_ref.shape[1]
    half = tm // 2

    def activation(g_ref, u_ref, cwg, cwu):
        def conv(buf_ref, cw):
            win = buf_ref[...]
            return (cw[0:1, :] * pltpu.roll(win, 2, 0)[SUBLANES:, :]
                    + cw[1:2, :] * pltpu.roll(win, 1, 0)[SUBLANES:, :]
                    + cw[2:3, :] * win[SUBLANES:, :] + cw[3:4, :])
        return (_silu(conv(g_ref, cwg)) * conv(u_ref, cwu)).astype(BF16)

    def act_cols(fi):
        return pl.ds(pl.multiple_of(fi * tf, tf), tf)

    @pl.when(f == 0)
    def _():
        h_ref[...] = _rms_rows(x_ref[...], nw_ref[...]).astype(BF16)

    @pl.when(jnp.logical_and(f == 0, m == 0))
    def _():
        carry_g_ref[...] = jnp.zeros_like(carry_g_ref)
        carry_u_ref[...] = jnp.zeros_like(carry_u_ref)
        bg_ref[...] = jnp.zeros_like(bg_ref)
        bu_ref[...] = jnp.zeros_like(bu_ref)

    def finish_previous(fi):
        act_ref[0:half, act_cols(fi)] = activation(ag_ref, au_ref, cgp_ref[...], cup_ref[...])
        act_ref[half:, act_cols(fi)] = activation(bg_ref, bu_ref, cgp_ref[...], cup_ref[...])

    @pl.when(f < nf)
    def _up():
        finish_previous(jnp.maximum(f - 1, 0))
        seq_start = m % tiles_per_seq == 0
        ag_ref[0:SUBLANES, :] = jnp.where(seq_start, 0.0, carry_g_ref[f])
        au_ref[0:SUBLANES, :] = jnp.where(seq_start, 0.0, carry_u_ref[f])
        h_a = h_ref[0:half, :]
        ag_ref[SUBLANES:, :] = _dot(h_a, wg_ref[...])
        au_ref[SUBLANES:, :] = _dot(h_a, wu_ref[...])
        h_b = h_ref[half:, :]
        bg_ref[0:SUBLANES, :] = ag_ref[half:half + SUBLANES, :]
        bu_ref[0:SUBLANES, :] = au_ref[half:half + SUBLANES, :]
        bg_ref[SUBLANES:, :] = _dot(h_b, wg_ref[...])
        bu_ref[SUBLANES:, :] = _dot(h_b, wu_ref[...])
        carry_g_ref[f] = bg_ref[half:half + SUBLANES, :]
        carry_u_ref[f] = bu_ref[half:half + SUBLANES, :]

    @pl.when(f == nf)
    def _():
        finish_previous(nf - 1)

    @pl.when(f >= nf)
    def _down():
        o_ref[...] = xres_ref[...] + _dot(act_ref[...], wd_ref[...])


def _ffn(x, norm_w, w_up, cw, w_down, layer, seq_len, tm=1024, tf=512, td=256):
    T, D = x.shape
    F = w_down.shape[0]
    assert T % tm == 0 and F % tf == 0 and seq_len % tm == 0 and D % td == 0 and (tm // 2) % SUBLANES == 0
    nf = F // tf
    nd = D // td
    cur = lambda f: jnp.minimum(f, nf - 1)
    prev = lambda f: jnp.maximum(jnp.minimum(f, nf) - 1, 0)
    gate_blk = lambda m, f: (0, cur(f))
    up_blk = lambda m, f: (0, nf + cur(f))
    cgate_blk = lambda m, f: (layer, 0, cur(f))
    cup_blk = lambda m, f: (layer, 0, nf + cur(f))
    gate_prev = lambda m, f: (layer, 0, prev(f))
    up_prev = lambda m, f: (layer, 0, nf + prev(f))
    down_blk = lambda m, f: (0, jnp.maximum(f - nf, 0))
    out_blk = lambda m, f: (m, jnp.maximum(f - nf, 0))
    stage = pltpu.VMEM((SUBLANES + tm // 2, tf), F32)
    return pl.pallas_call(
        functools.partial(_ffn_body, tiles_per_seq=seq_len // tm, nf=nf),
        grid=(T // tm, nf + nd),
        in_specs=[
            pl.BlockSpec((tm, D), lambda m, f: (m, 0)),
            pl.BlockSpec((1, D), lambda m, f: (0, 0)),
            pl.BlockSpec((D, tf), gate_blk),
            pl.BlockSpec((D, tf), up_blk),
            pl.BlockSpec((None, SUBLANES, tf), cgate_blk),
            pl.BlockSpec((None, SUBLANES, tf), cup_blk),
            pl.BlockSpec((None, SUBLANES, tf), gate_prev),
            pl.BlockSpec((None, SUBLANES, tf), up_prev),
            pl.BlockSpec((F, td), down_blk),
            pl.BlockSpec((tm, td), out_blk),
        ],
        out_specs=pl.BlockSpec((tm, td), out_blk),
        out_shape=jax.ShapeDtypeStruct((T, D), F32),
        scratch_shapes=[
            pltpu.VMEM((tm, D), BF16),
            pltpu.VMEM((tm, F), BF16),
            stage, stage, stage, stage,
            pltpu.VMEM((nf, SUBLANES, tf), F32),
            pltpu.VMEM((nf, SUBLANES, tf), F32),
        ],
        compiler_params=_cparams("arbitrary", "arbitrary"),
        name="conv_ffn",
    )(x, norm_w.reshape(1, D).astype(F32), w_up, w_up, cw, cw, cw, cw, w_down, x)


def _cast_spec(src, layer, axis, max_chunks):
    shape = src.shape[1:]
    for width in (LANES, 2 * LANES, 4 * LANES, 8 * LANES):
        if shape[axis] % width == 0 and shape[axis] // width <= max_chunks:
            return src, layer, ((width, shape[1]) if axis == 0 else (shape[0], width))
    raise ValueError("no chunking of the cast source fits the host's grid")


def _conv_rows(w, b):
    K, C = w.shape[-2:]
    pad = jnp.zeros(w.shape[:-2] + (SUBLANES - K - 1, C), F32)
    return jnp.concatenate([w.astype(F32), b.astype(F32)[..., None, :], pad], axis=-2)


def _mla_down_body(x_ref, nw_ref, w_ref, qn_ref, kvn_ref, cos_ref, sin_ref, cq_ref, ckv_ref, kpe_ref):
    h = _rms_rows(x_ref[...], nw_ref[...]).astype(BF16)
    r = _dot(h, w_ref[...])
    cq_ref[...] = _rms_rows(r[:, :MLA_Q_RANK], qn_ref[...]).astype(BF16)
    ckv_ref[...] = _rms_rows(r[:, MLA_Q_RANK:MLA_Q_RANK + MLA_KV_RANK], kvn_ref[...]).astype(BF16)
    g = r[:, MLA_Q_RANK + MLA_KV_RANK:]
    kpe_ref[...] = (g * cos_ref[...] + pltpu.roll(g, LANES // 2, 1) * sin_ref[...]).astype(BF16)


def _mla_down(x, norm_w, w, q_norm, kv_norm, cos, sin, tm=512):
    T, D = x.shape
    N = w.shape[1]
    row = lambda n: pl.BlockSpec((1, n), lambda m: (0, 0))
    tile = lambda n: pl.BlockSpec((tm, n), lambda m: (m, 0))
    return pl.pallas_call(
        _mla_down_body,
        grid=(T // tm,),
        in_specs=[tile(D), row(D), pl.BlockSpec((D, N), lambda m: (0, 0)), row(MLA_Q_RANK), row(MLA_KV_RANK),
                  tile(LANES), tile(LANES)],
        out_specs=[tile(MLA_Q_RANK), tile(MLA_KV_RANK), tile(LANES)],
        out_shape=[jax.ShapeDtypeStruct((T, MLA_Q_RANK), BF16), jax.ShapeDtypeStruct((T, MLA_KV_RANK), BF16),
                   jax.ShapeDtypeStruct((T, LANES), BF16)],
        compiler_params=_cparams("parallel"),
        name="mla_down",
    )(x, norm_w.reshape(1, D).astype(F32), w, q_norm.reshape(1, -1).astype(F32),
      kv_norm.reshape(1, -1).astype(F32), cos, sin)


def _lane_tile(x, width):
    return jnp.concatenate([x] * (width // LANES), axis=1)


def _mla_attn_body(q_ref, kn_ref, kpe_ref, v_ref, o_ref, kcat_ref, m_ref, l_ref, acc_ref, *, tk, hp):
    i = pl.program_id(2)
    tq = q_ref.shape[0]

    @pl.when(i == 0)
    def _():
        for h in range(hp):
            kcat_ref[h, :, :LANES] = kn_ref[:, h * LANES:(h + 1) * LANES]
            kcat_ref[h, :, LANES:] = kpe_ref[...]

    m_ref[...] = jnp.full(m_ref.shape, -jnp.inf, F32)
    l_ref[...] = jnp.zeros_like(l_ref)
    acc_ref[...] = jnp.zeros_like(acc_ref)
    qs = [q_ref[:, h * 2 * LANES:(h + 1) * 2 * LANES] for h in range(hp)]

    def step(j, masked):
        ks = pl.ds(pl.multiple_of(j * tk, tk), tk)
        ss = [_dot_nt(qs[h], kcat_ref[h, ks, :]) for h in range(hp)]
        for h in range(hp):
            s = ss[h]
            if masked:
                row = lax.broadcasted_iota(jnp.int32, (tq, tk), 0)
                col = lax.broadcasted_iota(jnp.int32, (tq, tk), 1)
                s = jnp.where(col <= row, s, -jnp.inf)
            m_prev = m_ref[h]
            m_new = jnp.maximum(m_prev, jnp.max(s, axis=-1, keepdims=True))
            alpha = jnp.exp2(m_prev - m_new)
            p = jnp.exp2(s - _lane_tile(m_new, tk))
            l_ref[h] = alpha * l_ref[h] + jnp.sum(p, axis=-1, keepdims=True)
            acc_ref[h] = alpha * acc_ref[h] + _dot(p.astype(BF16), v_ref[ks, h * LANES:(h + 1) * LANES])
            m_ref[h] = m_new

    def body(j, c):
        step(j, False)
        return c

    lax.fori_loop(0, i, body, 0)
    step(i, True)
    for h in range(hp):
        o_ref[:, h * LANES:(h + 1) * LANES] = (acc_ref[h] / l_ref[h]).astype(o_ref.dtype)


def _mla_attn(q, kv, kpe, batch, seq_len, tq=512, hp=4):
    T = q.shape[0]
    H = MLA_HEADS
    assert MLA_V == LANES and MLA_NOPE == LANES and H % hp == 0
    nq = seq_len // tq
    ng = H // hp
    return pl.pallas_call(
        functools.partial(_mla_attn_body, tk=tq, hp=hp),
        grid=(batch, ng, nq),
        in_specs=[
            pl.BlockSpec((tq, hp * 2 * LANES), lambda b, g, i: (b * nq + i, g)),
            pl.BlockSpec((seq_len, hp * LANES), lambda b, g, i: (b, g)),
            pl.BlockSpec((seq_len, LANES), lambda b, g, i: (b, 0)),
            pl.BlockSpec((seq_len, hp * LANES), lambda b, g, i: (b, ng + g)),
        ],
        out_specs=pl.BlockSpec((tq, hp * LANES), lambda b, g, i: (b * nq + i, g)),
        out_shape=jax.ShapeDtypeStruct((T, H * MLA_V), BF16),
        scratch_shapes=[
            pltpu.VMEM((hp, seq_len, 2 * LANES), BF16),
            pltpu.VMEM((hp, tq, LANES), F32),
            pltpu.VMEM((hp, tq, LANES), F32),
            pltpu.VMEM((hp, tq, LANES), F32),
        ],
        compiler_params=_cparams("parallel", "parallel", "arbitrary"),
        name="mla_attn",
    )(q, kv, kpe, kv)


def _mla_layer(x, norm_w, tabs, wq_a, q_norm, wq_b, wkv_a, kv_norm, wkv_b, wo, batch, seq_len,
               ffn_w_up, ffn_w_down, layer):
    cos_a, sin_a = tabs[0], tabs[1]
    D = x.shape[1]
    H = MLA_HEADS
    half = MLA_ROPE // 2
    scale = (MLA_NOPE + MLA_ROPE) ** -0.5 * LOG2_E
    zk = jnp.zeros((D, half), F32)
    w_down = jnp.concatenate(
        [wq_a, wkv_a[:, :MLA_KV_RANK], wkv_a[:, MLA_KV_RANK:MLA_KV_RANK + half], zk,
         wkv_a[:, MLA_KV_RANK + half:], zk], axis=1).astype(BF16)
    wq = wq_b.reshape(MLA_Q_RANK, H, MLA_NOPE + MLA_ROPE)
    zq = jnp.zeros((MLA_Q_RANK, H, half), F32)
    wq = jnp.concatenate([wq[:, :, :MLA_NOPE], wq[:, :, MLA_NOPE:MLA_NOPE + half], zq,
                          wq[:, :, MLA_NOPE + half:], zq], axis=2)
    wq = (wq * scale).reshape(MLA_Q_RANK, H * 2 * LANES).astype(BF16)
    wkv = wkv_b.reshape(MLA_KV_RANK, H, 2, MLA_NOPE).transpose(0, 2, 1, 3).reshape(MLA_KV_RANK, 2 * H * MLA_NOPE)
    wkv = wkv.astype(BF16)

    cq, ckv, kpe = _mla_down(x, norm_w, w_down, q_norm, kv_norm, cos_a, sin_a)
    q, ffn_up = _proj(cq, wq, epi="rope_group", cos=cos_a, sin=sin_a, out_dtype=BF16,
                      casts=(_cast_spec(ffn_w_up, layer, 1, 32),), name="mla_q")
    kv, ffn_down = _proj(ckv, wkv, out_dtype=BF16, casts=(_cast_spec(ffn_w_down, layer, 0, 32),), name="mla_kv")
    o = _mla_attn(q, kv, kpe, batch, seq_len)
    return _proj(o, wo.astype(BF16), res=x, name="mla_out"), ffn_up, ffn_down


def _sb_attn_body(q_ref, k_ref, v_ref, after_ref, o_ref, acc_ref, drop_ref, *, tk, sub, hp):
    i = pl.program_id(2)
    tq = q_ref.shape[0]
    acc_ref[...] = jnp.zeros_like(acc_ref)
    drop_ref[...] = jnp.zeros_like(drop_ref)
    qs = [q_ref[:, h * LANES:(h + 1) * LANES] for h in range(hp)]
    after = after_ref[...]

    def step(j, masked):
        ks = pl.ds(pl.multiple_of(j * tk, tk), tk)
        zs = [_dot_nt(qs[h], k_ref[ks, h * LANES:(h + 1) * LANES]) for h in range(hp)]
        for h in range(hp):
            z = zs[h]
            sp = jnp.maximum(z, 0.0) + jnp.log2(1.0 + jnp.exp2(_neg_abs(z)))
            if masked:
                row = lax.broadcasted_iota(jnp.int32, (tq, tk), 0)
                col = lax.broadcasted_iota(jnp.int32, (tq, tk), 1)
                strict = col < row
                sp = jnp.where(strict, sp, 0.0)
            drop = drop_ref[h]
            parts = [None] * (tk // sub)
            for b in reversed(range(tk // sub)):
                cs = slice(b * sub, (b + 1) * sub)
                sp_b = sp[:, cs]
                within = _dot(sp_b.astype(BF16), after)
                a_b = jnp.exp2(z[:, cs] - sp_b - within - _lane_tile(drop, sub))
                if masked:
                    a_b = jnp.where(strict[:, cs], a_b, 0.0)
                parts[b] = a_b.astype(BF16)
                drop = drop + jnp.sum(sp_b, axis=-1, keepdims=True)
            acc_ref[h] += _dot(jnp.concatenate(parts, axis=1), v_ref[ks, h * LANES:(h + 1) * LANES])
            drop_ref[h] = drop

    step(i, True)

    def live():
        return jnp.min(drop_ref[...]) < SB_DEAD_LOG2

    def body(carry):
        t, _ = carry
        step(i - 1 - t, False)
        return t + 1, live()

    lax.while_loop(lambda c: jnp.logical_and(c[0] < i, c[1]), body, (jnp.int32(0), live()))
    for h in range(hp):
        o_ref[:, h * LANES:(h + 1) * LANES] = acc_ref[h].astype(o_ref.dtype)


def _sb_attn(qkv, batch, seq_len, tq=256, sub=256, hp=8):
    T = qkv.shape[0]
    H, Dh = SB_HEADS, SB_HEAD_DIM
    assert Dh == LANES and H % hp == 0
    nq = seq_len // tq
    ng = H // hp
    after = (jnp.arange(sub)[:, None] > jnp.arange(sub)[None, :]).astype(BF16)
    return pl.pallas_call(
        functools.partial(_sb_attn_body, tk=tq, sub=sub, hp=hp),
        grid=(batch, ng, nq),
        in_specs=[
            pl.BlockSpec((tq, hp * Dh), lambda b, g, i: (b * nq + i, g)),
            pl.BlockSpec((seq_len, hp * Dh), lambda b, g, i: (b, ng + g)),
            pl.BlockSpec((seq_len, hp * Dh), lambda b, g, i: (b, 2 * ng + g)),
            pl.BlockSpec((sub, sub), lambda b, g, i: (0, 0)),
        ],
        out_specs=pl.BlockSpec((tq, hp * Dh), lambda b, g, i: (b * nq + i, g)),
        out_shape=jax.ShapeDtypeStruct((T, H * Dh), BF16),
        scratch_shapes=[pltpu.VMEM((hp, tq, Dh), F32), pltpu.VMEM((hp, tq, LANES), F32)],
        compiler_params=_cparams("parallel", "parallel", "arbitrary"),
        name="sb_attn",
    )(qkv, qkv, qkv, after)


def _sb_layer(x, norm_w, wqkv, wo, batch, seq_len, ffn_w_up, ffn_w_down, layer):
    n_q = SB_HEADS * SB_HEAD_DIM
    q_scale = SB_HEAD_DIM ** -0.5 * LOG2_E
    col_scale = jnp.where(jnp.arange(wqkv.shape[1]) < n_q, q_scale, 1.0).astype(F32)
    w = (wqkv * col_scale[None, :]).astype(BF16)
    qkv, ffn_up, ffn_down = _proj(
        x, w, norm_w=norm_w, out_dtype=BF16, name="sb_qkv",
        casts=(_cast_spec(ffn_w_up, layer, 1, 48), _cast_spec(ffn_w_down, layer, 0, 48)))
    o = _sb_attn(qkv, batch, seq_len)
    return _proj(o, wo.astype(BF16), res=x, name="sb_out"), ffn_up, ffn_down


def _ssd_body(zx_ref, dt_ref, cw_ref, dtb_ref, alog_ref, dfull_ref, nw_ref, exp_ref, o_ref,
              state_ref, carry_ref, buf_ref, y_ref, *, d_inner):
    c = pl.program_id(1)
    L = zx_ref.shape[0]
    G, N, P = SSD_GROUPS, SSD_STATE, SSD_HEAD_DIM
    gw = d_inner // G
    conv_ch = d_inner + 2 * G * N

    @pl.when(c == 0)
    def _():
        state_ref[...] = jnp.zeros_like(state_ref)
        carry_ref[...] = jnp.zeros_like(carry_ref)

    u = zx_ref[:, d_inner:d_inner + conv_ch]
    buf_ref[0:SUBLANES, :] = carry_ref[...]
    buf_ref[SUBLANES:SUBLANES + L, :] = u
    carry_ref[...] = u[L - SUBLANES:L, :]
    cw = cw_ref[...]
    conv = cw[SSD_CONV:SSD_CONV + 1, :] + cw[SSD_CONV - 1:SSD_CONV, :] * u
    win = buf_ref[...]
    for k in range(SSD_CONV - 1):
        conv = conv + cw[k:k + 1, :] * pltpu.roll(win, SSD_CONV - 1 - k, 0)[SUBLANES:, :]
    xbc = _silu(conv)
    xs = xbc[:, :d_inner]
    bm = xbc[:, d_inner:d_inner + G * N]
    cm = xbc[:, d_inner + G * N:]

    dt = _softplus(dt_ref[...] + dtb_ref[...])
    a = dt * (-jnp.exp(alog_ref[...]))
    r_i = lax.broadcasted_iota(jnp.int32, (L, L), 0)
    c_i = lax.broadcasted_iota(jnp.int32, (L, L), 1)
    causal = c_i <= r_i
    acum_t = _dot_split(a.T, jnp.where(c_i >= r_i, 1.0, 0.0).astype(BF16))
    acum = acum_t.T
    expand = exp_ref[...]
    dt_full = _dot_split(dt, expand)
    acum_full = _dot_split(acum, expand)
    last_full = acum_full[L - 1:L, :]
    xdt = xs * dt_full
    xdt_bf = xdt.astype(BF16)
    xdec_bf = (xdt * jnp.exp(last_full - acum_full)).astype(BF16)
    grow = jnp.exp(acum_full)
    chunk_decay = jnp.exp(last_full)
    lane = lax.broadcasted_iota(jnp.int32, (L, LANES), 1)
    first_half = lane < P

    for g in range(G):
        b_g = bm[:, g * N:(g + 1) * N]
        c_g = cm[:, g * N:(g + 1) * N].astype(BF16)
        cb = _dot_nt(c_g, b_g.astype(BF16))
        prev = state_ref[g]
        gs = slice(g * gw, (g + 1) * gw)
        y_off = _dot(c_g, prev.astype(BF16)) * grow[:, gs]
        parts = []
        for pair in range(gw // LANES):
            h0 = g * (gw // P) + 2 * pair
            xp = xdt_bf[:, h0 * P:h0 * P + LANES]
            ms = []
            for h in (h0, h0 + 1):
                seg = acum[:, h:h + 1] - acum_t[h:h + 1, :]
                ms.append((cb * jnp.exp(jnp.where(causal, seg, -jnp.inf))).astype(BF16))
            zero = jnp.zeros_like(xp)
            parts.append(_dot(ms[0], jnp.where(first_half, xp, zero)) + _dot(ms[1], jnp.where(first_half, zero, xp)))
        y_ref[:, gs] = jnp.concatenate(parts, axis=1) + y_off
        state_ref[g] = prev * chunk_decay[:, gs] + _dot(b_g.T.astype(BF16), xdec_bf[:, gs])

    y = (y_ref[...] + dfull_ref[...] * xs) * _silu(zx_ref[:, :d_inner])
    nw = nw_ref[...]
    for g in range(G):
        gs = slice(g * gw, (g + 1) * gw)
        o_ref[:, gs] = _rms_rows(y[:, gs], nw[:, gs]).astype(o_ref.dtype)


def _ssd_core(zx, dt_raw, cw, dt_bias, a_log, d_full, norm_w, expand, batch, seq_len, chunk=128):
    T = zx.shape[0]
    d_inner = d_full.shape[1]
    conv_ch = cw.shape[1]
    nc = seq_len // chunk
    G, N = SSD_GROUPS, SSD_STATE
    row = lambda n: pl.BlockSpec((1, n), lambda b, c: (0, 0))
    return pl.pallas_call(
        functools.partial(_ssd_body, d_inner=d_inner),
        grid=(batch, nc),
        in_specs=[
            pl.BlockSpec((chunk, zx.shape[1]), lambda b, c: (b * nc + c, 0)),
            pl.BlockSpec((chunk, LANES), lambda b, c: (b * nc + c, 0)),
            pl.BlockSpec((SUBLANES, conv_ch), lambda b, c: (0, 0)),
            row(LANES), row(LANES), row(d_inner), row(d_inner),
            pl.BlockSpec((LANES, d_inner), lambda b, c: (0, 0)),
        ],
        out_specs=pl.BlockSpec((chunk, d_inner), lambda b, c: (b * nc + c, 0)),
        out_shape=jax.ShapeDtypeStruct((T, d_inner), BF16),
        scratch_shapes=[
            pltpu.VMEM((G, N, d_inner // G), F32),
            pltpu.VMEM((SUBLANES, conv_ch), F32),
            pltpu.VMEM((SUBLANES + chunk, conv_ch), F32),
            pltpu.VMEM((chunk, d_inner), F32),
        ],
        compiler_params=_cparams("parallel", "arbitrary"),
        name="ssd_core",
    )(zx, dt_raw, cw, dt_bias, a_log, d_full, norm_w, expand)


def _ssd_layer(x, norm_w, w_in, conv_w, conv_b, dt_bias, a_log, d_skip, ssd_norm, w_out, batch, seq_len,
               ffn_w_up, ffn_w_down, layer):
    heads = d_skip.shape[0]
    d_inner = heads * SSD_HEAD_DIM
    n_zx = w_in.shape[1] - heads
    assert heads <= LANES
    pad = lambda v: jnp.concatenate([v.astype(F32), jnp.zeros((LANES - heads,), F32)]).reshape(1, LANES)
    w_dt = jnp.concatenate([w_in[:, n_zx:], jnp.zeros((w_in.shape[0], LANES - heads), F32)], axis=1).astype(BF16)
    zx, ffn_up = _proj(x, w_in.astype(BF16), cols=(0, n_zx), norm_w=norm_w, name="ssd_in",
                       casts=(_cast_spec(ffn_w_up, layer, 1, 64),))
    dt_raw = _proj(x, w_dt, norm_w=norm_w, name="ssd_dt")
    expand = (jnp.arange(LANES)[:, None] == (jnp.arange(d_inner) // SSD_HEAD_DIM)[None, :]).astype(BF16)
    d_full = jnp.repeat(d_skip.astype(F32), SSD_HEAD_DIM).reshape(1, d_inner)
    y = _ssd_core(zx, dt_raw, _conv_rows(conv_w, conv_b), pad(dt_bias), pad(a_log), d_full,
                  ssd_norm.reshape(1, d_inner).astype(F32), expand, batch, seq_len)
    x, ffn_down = _proj(y, w_out.astype(BF16), res=x, name="ssd_out", casts=(_cast_spec(ffn_w_down, layer, 0, 32),))
    return x, ffn_up, ffn_down


def _ret_body(qk_ref, v_ref, g_ref, o_ref, state_ref, *, heads, dk, dv):
    c = pl.program_id(1)
    L = qk_ref.shape[0]

    @pl.when(c == 0)
    def _():
        state_ref[...] = jnp.zeros_like(state_ref)

    r_i = lax.broadcasted_iota(jnp.int32, (L, L), 0)
    c_i = lax.broadcasted_iota(jnp.int32, (L, L), 1)
    diff = (r_i - c_i).astype(F32)
    lower = diff >= 0
    idx = lax.broadcasted_iota(jnp.int32, (L, 1), 0).astype(F32)

    for h in range(heads):
        log_g = math.log(1.0 - 2.0 ** (-5.0 - h))
        d_intra = jnp.where(lower, jnp.exp(jnp.maximum(diff, 0.0) * log_g), 0.0)
        k_dec = jnp.exp((L - 1.0 - idx) * log_g)
        q_dec = jnp.exp((idx + 1.0) * log_g)
        q = qk_ref[:, h * dk:(h + 1) * dk]
        k = qk_ref[:, (heads + h) * dk:(heads + h + 1) * dk]
        vs = slice(h * dv, (h + 1) * dv)
        v = v_ref[:, vs]
        prev = state_ref[h]
        scores = (_dot_nt(q, k) * d_intra).astype(BF16)
        o = _dot(scores, v) + q_dec * _dot(q, prev.astype(BF16))
        kd_t = (k.astype(F32) * k_dec).T.astype(BF16)
        state_ref[h] = math.exp(L * log_g) * prev + _dot(kd_t, v)
        ms = jnp.mean(o * o, axis=-1, keepdims=True)
        o_ref[:, vs] = (_silu(g_ref[:, vs]) * (o * lax.rsqrt(ms + RMS_EPS))).astype(o_ref.dtype)


def _ret_core(qk, v, g, batch, seq_len, dk, dv, chunk=256):
    T = qk.shape[0]
    H = RET_HEADS
    nc = seq_len // chunk
    blk = lambda n: pl.BlockSpec((chunk, n), lambda b, c: (b * nc + c, 0))
    return pl.pallas_call(
        functools.partial(_ret_body, heads=H, dk=dk, dv=dv),
        grid=(batch, nc),
        in_specs=[blk(2 * H * dk), blk(H * dv), blk(H * dv)],
        out_specs=blk(H * dv),
        out_shape=jax.ShapeDtypeStruct((T, H * dv), BF16),
        scratch_shapes=[pltpu.VMEM((H, dk, dv), F32)],
        compiler_params=_cparams("parallel", "arbitrary"),
        name="ret_core",
    )(qk, v, g)


def _ret_layer(x, norm_w, tabs, w_in, wo, batch, seq_len, ffn_w_up, ffn_w_down, layer):
    cos_r, sin_r = tabs[2], tabs[3]
    D = x.shape[1]
    H = RET_HEADS
    dk = D // H
    dv = 2 * D // H
    n_qk = 2 * H * dk
    n_v = H * dv
    n_all = w_in.shape[1]
    col = jnp.arange(n_all)
    col_scale = jnp.where((col >= H * dk) & (col < n_qk), dk ** -0.5, 1.0).astype(F32)
    w = (w_in * col_scale[None, :]).astype(BF16)
    qk, ffn_up = _proj(x, w, cols=(0, n_qk), norm_w=norm_w, epi="rope_half", cos=cos_r, sin=sin_r, out_dtype=BF16,
                       casts=(_cast_spec(ffn_w_up, layer, 1, 32),), name="ret_qk")
    v, ffn_down = _proj(x, w, cols=(n_qk, n_v), norm_w=norm_w, out_dtype=BF16,
                        casts=(_cast_spec(ffn_w_down, layer, 0, 32),), name="ret_v")
    g = _proj(x, w, cols=(n_qk + n_v, n_all - n_qk - n_v), norm_w=norm_w, name="ret_g")
    o = _ret_core(qk, v, g, batch, seq_len, dk, dv)
    return _proj(o, wo.astype(BF16), res=x, name="ret_out"), ffn_up, ffn_down


def kernel(x, positions, norm_mix, norm_ffn, norm_final, mla_wq_a, mla_q_norm, mla_wq_b, mla_wkv_a, mla_kv_norm, mla_wkv_b, mla_wo, sb_wqkv, sb_wo, ssd_w_in, ssd_conv_w, ssd_conv_b, ssd_dt_bias, ssd_a_log, ssd_d, ssd_norm, ssd_w_out, ret_w_in, ret_wo, ffn_w_up, ffn_conv_w, ffn_conv_b, ffn_w_down):
    B, S, D = x.shape
    depth = norm_mix.shape[0]
    n_mixers = 4
    tabs = _rope_tables(positions, D // RET_HEADS)
    cw = _conv_rows(ffn_conv_w, ffn_conv_b)
    xt = x.reshape(B * S, D)
    for i in range(depth):
        m, j = i % n_mixers, i // n_mixers
        ffn = (ffn_w_up, ffn_w_down, i)
        if m == 0:
            xt, w_up, w_down = _mla_layer(xt, norm_mix[i], tabs, mla_wq_a[j], mla_q_norm[j], mla_wq_b[j],
                                          mla_wkv_a[j], mla_kv_norm[j], mla_wkv_b[j], mla_wo[j], B, S, *ffn)
        elif m == 1:
            xt, w_up, w_down = _sb_layer(xt, norm_mix[i], sb_wqkv[j], sb_wo[j], B, S, *ffn)
        elif m == 2:
            xt, w_up, w_down = _ssd_layer(xt, norm_mix[i], ssd_w_in[j], ssd_conv_w[j], ssd_conv_b[j],
                                          ssd_dt_bias[j], ssd_a_log[j], ssd_d[j], ssd_norm[j], ssd_w_out[j],
                                          B, S, *ffn)
        else:
            xt, w_up, w_down = _ret_layer(xt, norm_mix[i], tabs, ret_w_in[j], ret_wo[j], B, S, *ffn)
        xt = _ffn(xt, norm_ffn[i], w_up, cw, w_down, i, S)
    return _rmsnorm(xt, norm_final).reshape(B, S, D)
```

```python
import functools
import math

import jax
import jax.numpy as jnp
from jax import lax
from jax.experimental import pallas as pl
from jax.experimental.pallas import tpu as pltpu

F32 = jnp.float32
BF16 = jnp.bfloat16

RMS_EPS = 1e-6
ROPE_BASE = 10000.0
LOG2_E = 1.4426950408889634
SB_DEAD_LOG2 = 160.0

MLA_HEADS = 16
MLA_Q_RANK = 512
MLA_KV_RANK = 512
MLA_NOPE = 128
MLA_ROPE = 64
MLA_V = 128
SB_HEADS = 16
SB_HEAD_DIM = 128
SSD_HEAD_DIM = 64
SSD_GROUPS = 8
SSD_STATE = 128
SSD_CONV = 4
RET_HEADS = 8
FFN_CONV = 3

LANES = 128
SUBLANES = 8
VMEM_LIMIT_BYTES = 56 * 1024 * 1024
PROJ_VMEM_BUDGET_BYTES = 48 * 1024 * 1024


def _cparams(*sem):
    return pltpu.CompilerParams(dimension_semantics=sem, vmem_limit_bytes=VMEM_LIMIT_BYTES)


def _dot(a, b):
    return jnp.dot(a, b, preferred_element_type=F32)


def _dot_nt(a, b):
    return lax.dot_general(a, b, (((1,), (1,)), ((), ())), preferred_element_type=F32)


def _dot_split(a, b):
    hi = a.astype(BF16)
    lo = (a - hi.astype(F32)).astype(BF16)
    return _dot(hi, b) + _dot(lo, b)


def _rms_rows(xf, w):
    ms = jnp.mean(xf * xf, axis=-1, keepdims=True)
    return xf * lax.rsqrt(ms + RMS_EPS) * w


def _silu(x):
    return x * jax.nn.sigmoid(x)


def _neg_abs(x):
    bits = lax.bitcast_convert_type(x, jnp.uint32) | jnp.uint32(0x80000000)
    return lax.bitcast_convert_type(bits, F32)


def _softplus(x):
    return jnp.maximum(x, 0.0) + jnp.log(1.0 + jnp.exp(-jnp.abs(x)))


def _rope_tables_body(pos_ref, inv_a_ref, sgn_a_ref, inv_r_ref, cos_a_ref, sin_a_ref, cos_r_ref, sin_r_ref):
    pos = pos_ref[...]
    ang_a = pos * inv_a_ref[...]
    cos_a_ref[...] = jnp.cos(ang_a)
    sin_a_ref[...] = jnp.sin(ang_a) * sgn_a_ref[...]
    ang_r = pos * inv_r_ref[...]
    cos_r_ref[...] = jnp.cos(ang_r)
    sin_r_ref[...] = jnp.sin(ang_r)


def _rope_tables(positions, ret_dk):
    T = positions.size
    tm = 1024
    pos = positions.reshape(T, 1).astype(F32)
    half_a = MLA_ROPE // 2
    inv_a = 1.0 / (ROPE_BASE ** (jnp.arange(half_a, dtype=F32) * (2.0 / MLA_ROPE)))
    z = jnp.zeros((half_a,), F32)
    inv_a = jnp.concatenate([inv_a, z, inv_a, z]).reshape(1, LANES)
    o = jnp.ones((2 * half_a,), F32)
    sgn_a = jnp.concatenate([-o, o]).reshape(1, LANES)
    half_r = ret_dk // 2
    assert half_r == LANES
    inv_r = (1.0 / (ROPE_BASE ** (jnp.arange(half_r, dtype=F32) * (2.0 / ret_dk)))).reshape(1, LANES)
    row = pl.BlockSpec((1, LANES), lambda i: (0, 0))
    tab = pl.BlockSpec((tm, LANES), lambda i: (i, 0))
    shp = jax.ShapeDtypeStruct((T, LANES), F32)
    return pl.pallas_call(
        _rope_tables_body,
        grid=(T // tm,),
        in_specs=[pl.BlockSpec((tm, 1), lambda i: (i, 0)), row, row, row],
        out_specs=[tab, tab, tab, tab],
        out_shape=[shp, shp, shp, shp],
        compiler_params=_cparams("parallel"),
        name="rope_tables",
    )(pos, inv_a, sgn_a, inv_r)


def _proj_body(*refs, has_norm, epi, has_res, n_casts):
    it = iter(refs)
    a_ref = next(it)
    nw_ref = next(it) if has_norm else None
    w_ref = next(it)
    res_ref = next(it) if has_res else None
    cos_ref = sin_ref = None
    if epi is not None:
        cos_ref = next(it)
        sin_ref = next(it)
    cast_src = [next(it) for _ in range(n_casts)]
    o_ref = next(it)
    cast_dst = [next(it) for _ in range(n_casts)]
    abf_ref = next(it) if has_norm else None

    for src_ref, dst_ref in zip(cast_src, cast_dst):
        dst_ref[...] = src_ref[...].astype(BF16)

    if has_norm:
        @pl.when(pl.program_id(1) == 0)
        def _():
            abf_ref[...] = _rms_rows(a_ref[...].astype(F32), nw_ref[...]).astype(BF16)
        a = abf_ref[...]
    else:
        a = a_ref[...]
    acc = _dot(a, w_ref[...])
    if epi == "rope_half":
        c = cos_ref[...]
        s = sin_ref[...]
        parts = []
        for j in range(acc.shape[1] // (2 * LANES)):
            x1 = acc[:, (2 * j) * LANES:(2 * j + 1) * LANES]
            x2 = acc[:, (2 * j + 1) * LANES:(2 * j + 2) * LANES]
            parts += [x1 * c - x2 * s, x2 * c + x1 * s]
        acc = jnp.concatenate(parts, axis=1)
    elif epi == "rope_group":
        c = cos_ref[...]
        s = sin_ref[...]
        parts = []
        for j in range(acc.shape[1] // (2 * LANES)):
            g = acc[:, (2 * j + 1) * LANES:(2 * j + 2) * LANES]
            parts += [acc[:, (2 * j) * LANES:(2 * j + 1) * LANES], g * c + pltpu.roll(g, LANES // 2, 1) * s]
        acc = jnp.concatenate(parts, axis=1)
    if has_res:
        acc = acc + res_ref[...]
    o_ref[...] = acc.astype(o_ref.dtype)


def _proj_tiles(T, K, N, a_bytes, out_bytes, norm_bufs, has_res, side_bytes, min_steps):
    for tm in (1024, 512, 256):
        if T % tm:
            continue
        for tn in (1024, 512, 256, LANES):
            if N % tn or (T // tm) * (N // tn) < min_steps:
                continue
            need = (2 * tm * K * a_bytes + norm_bufs * tm * K * 2 + 2 * K * tn * 2 + side_bytes
                    + 2 * tm * tn * out_bytes + (2 * tm * tn * 4 if has_res else 0) + tm * tn * 4)
            if need <= PROJ_VMEM_BUDGET_BYTES:
                return tm, tn
    raise ValueError("no projection tiling fits VMEM")


def _proj(a, w, *, cols=None, norm_w=None, res=None, epi=None, cos=None, sin=None, out_dtype=F32, casts=(),
          emit_norm=False, name="proj"):
    T, K = a.shape
    c0, N = (0, w.shape[1]) if cols is None else cols
    has_norm = norm_w is not None
    has_res = res is not None
    assert has_norm or not emit_norm
    side_bytes = sum(2 * br * bc * (4 + 2) for _, _, (br, bc) in casts)
    min_steps = max([(s.shape[1] // br) * (s.shape[2] // bc) for s, _, (br, bc) in casts], default=1)
    tm, tn = _proj_tiles(T, K, N, a.dtype.itemsize, jnp.dtype(out_dtype).itemsize, has_norm + emit_norm, has_res,
                         side_bytes, min_steps)
    assert c0 % tn == 0
    nb0 = c0 // tn
    in_specs = [pl.BlockSpec((tm, K), lambda m, n: (m, 0))]
    args = [a]
    if has_norm:
        in_specs.append(pl.BlockSpec((1, K), lambda m, n: (0, 0)))
        args.append(norm_w.reshape(1, K).astype(F32))
    in_specs.append(pl.BlockSpec((K, tn), lambda m, n: (0, nb0 + n)))
    args.append(w)
    if has_res:
        in_specs.append(pl.BlockSpec((tm, tn), lambda m, n: (m, n)))
        args.append(res)
    if epi is not None:
        in_specs += [pl.BlockSpec((tm, LANES), lambda m, n: (m, 0))] * 2
        args += [cos, sin]
    scratch = [pltpu.VMEM((tm, K), BF16)] if has_norm and not emit_norm else []
    out_specs = [pl.BlockSpec((tm, tn), lambda m, n: (m, n))]
    out_shape = [jax.ShapeDtypeStruct((T, N), out_dtype)]
    n_n = N // tn
    n_steps = (T // tm) * n_n
    for src, layer, (br, bc) in casts:
        _, R, C = src.shape
        assert R % br == 0 and C % bc == 0 and (R // br) * (C // bc) <= n_steps
        ncc, last = C // bc, (R // br) * (C // bc) - 1

        def chunk(m, n, ncc=ncc, last=last):
            c = jnp.minimum(m * n_n + n, last)
            return c // ncc, c % ncc

        in_specs.append(pl.BlockSpec((None, br, bc), lambda m, n, layer=layer, chunk=chunk: (layer, *chunk(m, n))))
        args.append(src)
        out_specs.append(pl.BlockSpec((br, bc), chunk))
        out_shape.append(jax.ShapeDtypeStruct((R, C), BF16))
    if emit_norm:
        out_specs.append(pl.BlockSpec((tm, K), lambda m, n: (m, 0)))
        out_shape.append(jax.ShapeDtypeStruct((T, K), BF16))
    outs = pl.pallas_call(
        functools.partial(_proj_body, has_norm=has_norm, epi=epi, has_res=has_res, n_casts=len(casts)),
        grid=(T // tm, n_n),
        in_specs=in_specs,
        out_specs=out_specs,
        out_shape=out_shape,
        scratch_shapes=scratch,
        compiler_params=_cparams("arbitrary", "arbitrary"),
        name=name,
    )(*args)
    return outs if casts or emit_norm else outs[0]


def _rmsnorm_body(x_ref, w_ref, o_ref):
    o_ref[...] = _rms_rows(x_ref[...], w_ref[...])


def _rmsnorm(x, w, tm=512):
    T, D = x.shape
    return pl.pallas_call(
        _rmsnorm_body,
        grid=(T // tm,),
        in_specs=[pl.BlockSpec((tm, D), lambda m: (m, 0)), pl.BlockSpec((1, D), lambda m: (0, 0))],
        out_specs=pl.BlockSpec((tm, D), lambda m: (m, 0)),
        out_shape=jax.ShapeDtypeStruct((T, D), F32),
        compiler_params=_cparams("parallel"),
        name="final_norm",
    )(x, w.reshape(1, D).astype(F32))


def _ffn_body(x_ref, nw_ref, wg_ref, wu_ref, cg_ref, cu_ref, cgp_ref, cup_ref, wd_ref, xres_ref, o_ref,
              h_ref, act_ref, ag_ref, au_ref, bg_ref, bu_ref, carry_g_ref, carry_u_ref, *, tiles_per_seq, nf):
    m = pl.program_id(0)
    f = pl.program_id(1)
    tm = x_ref.shape[0]
    tf = wg_ref.shape[1]
    half = tm // 2

    def activation(g_ref, u_ref, cwg, cwu):
        def conv(buf_ref, cw):
            win = buf_ref[...]
            return (cw[0:1, :] * pltpu.roll(win, 2, 0)[SUBLANES:, :]
                    + cw[1:2, :] * pltpu.roll(win, 1, 0)[SUBLANES:, :]
                    + cw[2:3, :] * win[SUBLANES:, :] + cw[3:4, :])
        return (_silu(conv(g_ref, cwg)) * conv(u_ref, cwu)).astype(BF16)

    def act_cols(fi):
        return pl.ds(pl.multiple_of(fi * tf, tf), tf)

    @pl.when(f == 0)
    def _():
        h_ref[...] = _rms_rows(x_ref[...], nw_ref[...]).astype(BF16)

    @pl.when(jnp.logical_and(f == 0, m == 0))
    def _():
        carry_g_ref[...] = jnp.zeros_like(carry_g_ref)
        carry_u_ref[...] = jnp.zeros_like(carry_u_ref)
        bg_ref[...] = jnp.zeros_like(bg_ref)
        bu_ref[...] = jnp.zeros_like(bu_ref)

    def finish_previous(fi):
        act_ref[0:half, act_cols(fi)] = activation(ag_ref, au_ref, cgp_ref[...], cup_ref[...])
        act_ref[half:, act_cols(fi)] = activation(bg_ref, bu_ref, cgp_ref[...], cup_ref[...])

    @pl.when(f < nf)
    def _up():
        finish_previous(jnp.maximum(f - 1, 0))
        seq_start = m % tiles_per_seq == 0
        ag_ref[0:SUBLANES, :] = jnp.where(seq_start, 0.0, carry_g_ref[f])
        au_ref[0:SUBLANES, :] = jnp.where(seq_start, 0.0, carry_u_ref[f])
        h_a = h_ref[0:half, :]
        ag_ref[SUBLANES:, :] = _dot(h_a, wg_ref[...])
        au_ref[SUBLANES:, :] = _dot(h_a, wu_ref[...])
        h_b = h_ref[half:, :]
        bg_ref[0:SUBLANES, :] = ag_ref[half:half + SUBLANES, :]
        bu_ref[0:SUBLANES, :] = au_ref[half:half + SUBLANES, :]
        bg_ref[SUBLANES:, :] = _dot(h_b, wg_ref[...])
        bu_ref[SUBLANES:, :] = _dot(h_b, wu_ref[...])
        carry_g_ref[f] = bg_ref[half:half + SUBLANES, :]
        carry_u_ref[f] = bu_ref[half:half + SUBLANES, :]

    @pl.when(f == nf)
    def _():
        finish_previous(nf - 1)

    @pl.when(f >= nf)
    def _down():
        o_ref[...] = xres_ref[...] + _dot(act_ref[...], wd_ref[...])


def _ffn(x, norm_w, w_up, cw, w_down, layer, seq_len, tm=1024, tf=512, td=256):
    T, D = x.shape
    F = w_down.shape[0]
    assert T % tm == 0 and F % tf == 0 and seq_len % tm == 0 and D % td == 0 and (tm // 2) % SUBLANES == 0
    nf = F // tf
    nd = D // td
    cur = lambda f: jnp.minimum(f, nf - 1)
    prev = lambda f: jnp.maximum(jnp.minimum(f, nf) - 1, 0)
    gate_blk = lambda m, f: (0, cur(f))
    up_blk = lambda m, f: (0, nf + cur(f))
    cgate_blk = lambda m, f: (layer, 0, cur(f))
    cup_blk = lambda m, f: (layer, 0, nf + cur(f))
    gate_prev = lambda m, f: (layer, 0, prev(f))
    up_prev = lambda m, f: (layer, 0, nf + prev(f))
    down_blk = lambda m, f: (0, jnp.maximum(f - nf, 0))
    out_blk = lambda m, f: (m, jnp.maximum(f - nf, 0))
    stage = pltpu.VMEM((SUBLANES + tm // 2, tf), F32)
    return pl.pallas_call(
        functools.partial(_ffn_body, tiles_per_seq=seq_len // tm, nf=nf),
        grid=(T // tm, nf + nd),
        in_specs=[
            pl.BlockSpec((tm, D), lambda m, f: (m, 0)),
            pl.BlockSpec((1, D), lambda m, f: (0, 0)),
            pl.BlockSpec((D, tf), gate_blk),
            pl.BlockSpec((D, tf), up_blk),
            pl.BlockSpec((None, SUBLANES, tf), cgate_blk),
            pl.BlockSpec((None, SUBLANES, tf), cup_blk),
            pl.BlockSpec((None, SUBLANES, tf), gate_prev),
            pl.BlockSpec((None, SUBLANES, tf), up_prev),
            pl.BlockSpec((F, td), down_blk),
            pl.BlockSpec((tm, td), out_blk),
        ],
        out_specs=pl.BlockSpec((tm, td), out_blk),
        out_shape=jax.ShapeDtypeStruct((T, D), F32),
        scratch_shapes=[
            pltpu.VMEM((tm, D), BF16),
            pltpu.VMEM((tm, F), BF16),
            stage, stage, stage, stage,
            pltpu.VMEM((nf, SUBLANES, tf), F32),
            pltpu.VMEM((nf, SUBLANES, tf), F32),
        ],
        compiler_params=_cparams("arbitrary", "arbitrary"),
        name="conv_ffn",
    )(x, norm_w.reshape(1, D).astype(F32), w_up, w_up, cw, cw, cw, cw, w_down, x)


def _cast_spec(src, layer, axis, max_chunks):
    shape = src.shape[1:]
    for width in (LANES, 2 * LANES, 4 * LANES, 8 * LANES):
        if shape[axis] % width == 0 and shape[axis] // width <= max_chunks:
            return src, layer, ((width, shape[1]) if axis == 0 else (shape[0], width))
    raise ValueError("no chunking of the cast source fits the host's grid")


def _conv_rows(w, b):
    K, C = w.shape[-2:]
    pad = jnp.zeros(w.shape[:-2] + (SUBLANES - K - 1, C), F32)
    return jnp.concatenate([w.astype(F32), b.astype(F32)[..., None, :], pad], axis=-2)


def _mla_down_body(x_ref, nw_ref, w_ref, qn_ref, kvn_ref, cos_ref, sin_ref, cq_ref, ckv_ref, kpe_ref):
    h = _rms_rows(x_ref[...], nw_ref[...]).astype(BF16)
    r = _dot(h, w_ref[...])
    cq_ref[...] = _rms_rows(r[:, :MLA_Q_RANK], qn_ref[...]).astype(BF16)
    ckv_ref[...] = _rms_rows(r[:, MLA_Q_RANK:MLA_Q_RANK + MLA_KV_RANK], kvn_ref[...]).astype(BF16)
    g = r[:, MLA_Q_RANK + MLA_KV_RANK:]
    kpe_ref[...] = (g * cos_ref[...] + pltpu.roll(g, LANES // 2, 1) * sin_ref[...]).astype(BF16)


def _mla_down(x, norm_w, w, q_norm, kv_norm, cos, sin, tm=512):
    T, D = x.shape
    N = w.shape[1]
    row = lambda n: pl.BlockSpec((1, n), lambda m: (0, 0))
    tile = lambda n: pl.BlockSpec((tm, n), lambda m: (m, 0))
    return pl.pallas_call(
        _mla_down_body,
        grid=(T // tm,),
        in_specs=[tile(D), row(D), pl.BlockSpec((D, N), lambda m: (0, 0)), row(MLA_Q_RANK), row(MLA_KV_RANK),
                  tile(LANES), tile(LANES)],
        out_specs=[tile(MLA_Q_RANK), tile(MLA_KV_RANK), tile(LANES)],
        out_shape=[jax.ShapeDtypeStruct((T, MLA_Q_RANK), BF16), jax.ShapeDtypeStruct((T, MLA_KV_RANK), BF16),
                   jax.ShapeDtypeStruct((T, LANES), BF16)],
        compiler_params=_cparams("parallel"),
        name="mla_down",
    )(x, norm_w.reshape(1, D).astype(F32), w, q_norm.reshape(1, -1).astype(F32),
      kv_norm.reshape(1, -1).astype(F32), cos, sin)


def _lane_tile(x, width):
    return jnp.concatenate([x] * (width // LANES), axis=1)


def _mla_attn_body(q_ref, kn_ref, kpe_ref, v_ref, o_ref, kcat_ref, m_ref, l_ref, acc_ref, *, tk, hp):
    i = pl.program_id(2)
    tq = q_ref.shape[0]

    @pl.when(i == 0)
    def _():
        for h in range(hp):
            kcat_ref[h, :, :LANES] = kn_ref[:, h * LANES:(h + 1) * LANES]
            kcat_ref[h, :, LANES:] = kpe_ref[...]

    m_ref[...] = jnp.full(m_ref.shape, -jnp.inf, F32)
    l_ref[...] = jnp.zeros_like(l_ref)
    acc_ref[...] = jnp.zeros_like(acc_ref)
    qs = [q_ref[:, h * 2 * LANES:(h + 1) * 2 * LANES] for h in range(hp)]

    def step(j, masked):
        ks = pl.ds(pl.multiple_of(j * tk, tk), tk)
        ss = [_dot_nt(qs[h], kcat_ref[h, ks, :]) for h in range(hp)]
        for h in range(hp):
            s = ss[h]
            if masked:
                row = lax.broadcasted_iota(jnp.int32, (tq, tk), 0)
                col = lax.broadcasted_iota(jnp.int32, (tq, tk), 1)
                s = jnp.where(col <= row, s, -jnp.inf)
            m_prev = m_ref[h]
            m_new = jnp.maximum(m_prev, jnp.max(s, axis=-1, keepdims=True))
            alpha = jnp.exp2(m_prev - m_new)
            p = jnp.exp2(s - _lane_tile(m_new, tk))
            l_ref[h] = alpha * l_ref[h] + jnp.sum(p, axis=-1, keepdims=True)
            acc_ref[h] = alpha * acc_ref[h] + _dot(p.astype(BF16), v_ref[ks, h * LANES:(h + 1) * LANES])
            m_ref[h] = m_new

    def body(j, c):
        step(j, False)
        return c

    lax.fori_loop(0, i, body, 0)
    step(i, True)
    for h in range(hp):
        o_ref[:, h * LANES:(h + 1) * LANES] = (acc_ref[h] / l_ref[h]).astype(o_ref.dtype)


def _mla_attn(q, kv, kpe, batch, seq_len, tq=512, hp=4):
    T = q.shape[0]
    H = MLA_HEADS
    assert MLA_V == LANES and MLA_NOPE == LANES and H % hp == 0
    nq = seq_len // tq
    ng = H // hp
    return pl.pallas_call(
        functools.partial(_mla_attn_body, tk=tq, hp=hp),
        grid=(batch, ng, nq),
        in_specs=[
            pl.BlockSpec((tq, hp * 2 * LANES), lambda b, g, i: (b * nq + i, g)),
            pl.BlockSpec((seq_len, hp * LANES), lambda b, g, i: (b, g)),
            pl.BlockSpec((seq_len, LANES), lambda b, g, i: (b, 0)),
            pl.BlockSpec((seq_len, hp * LANES), lambda b, g, i: (b, ng + g)),
        ],
        out_specs=pl.BlockSpec((tq, hp * LANES), lambda b, g, i: (b * nq + i, g)),
        out_shape=jax.ShapeDtypeStruct((T, H * MLA_V), BF16),
        scratch_shapes=[
            pltpu.VMEM((hp, seq_len, 2 * LANES), BF16),
            pltpu.VMEM((hp, tq, LANES), F32),
            pltpu.VMEM((hp, tq, LANES), F32),
            pltpu.VMEM((hp, tq, LANES), F32),
        ],
        compiler_params=_cparams("parallel", "parallel", "arbitrary"),
        name="mla_attn",
    )(q, kv, kpe, kv)


def _mla_layer(x, norm_w, tabs, wq_a, q_norm, wq_b, wkv_a, kv_norm, wkv_b, wo, batch, seq_len,
               ffn_w_up, ffn_w_down, layer):
    cos_a, sin_a = tabs[0], tabs[1]
    D = x.shape[1]
    H = MLA_HEADS
    half = MLA_ROPE // 2
    scale = (MLA_NOPE + MLA_ROPE) ** -0.5 * LOG2_E
    zk = jnp.zeros((D, half), F32)
    w_down = jnp.concatenate(
        [wq_a, wkv_a[:, :MLA_KV_RANK], wkv_a[:, MLA_KV_RANK:MLA_KV_RANK + half], zk,
         wkv_a[:, MLA_KV_RANK + half:], zk], axis=1).astype(BF16)
    wq = wq_b.reshape(MLA_Q_RANK, H, MLA_NOPE + MLA_ROPE)
    zq = jnp.zeros((MLA_Q_RANK, H, half), F32)
    wq = jnp.concatenate([wq[:, :, :MLA_NOPE], wq[:, :, MLA_NOPE:MLA_NOPE + half], zq,
                          wq[:, :, MLA_NOPE + half:], zq], axis=2)
    wq = (wq * scale).reshape(MLA_Q_RANK, H * 2 * LANES).astype(BF16)
    wkv = wkv_b.reshape(MLA_KV_RANK, H, 2, MLA_NOPE).transpose(0, 2, 1, 3).reshape(MLA_KV_RANK, 2 * H * MLA_NOPE)
    wkv = wkv.astype(BF16)

    cq, ckv, kpe = _mla_down(x, norm_w, w_down, q_norm, kv_norm, cos_a, sin_a)
    q, ffn_up = _proj(cq, wq, epi="rope_group", cos=cos_a, sin=sin_a, out_dtype=BF16,
                      casts=(_cast_spec(ffn_w_up, layer, 1, 32),), name="mla_q")
    kv, ffn_down = _proj(ckv, wkv, out_dtype=BF16, casts=(_cast_spec(ffn_w_down, layer, 0, 32),), name="mla_kv")
    o = _mla_attn(q, kv, kpe, batch, seq_len)
    return _proj(o, wo.astype(BF16), res=x, name="mla_out"), ffn_up, ffn_down


def _sb_attn_body(q_ref, k_ref, v_ref, after_ref, o_ref, acc_ref, drop_ref, *, tk, sub, hp):
    i = pl.program_id(2)
    tq = q_ref.shape[0]
    acc_ref[...] = jnp.zeros_like(acc_ref)
    drop_ref[...] = jnp.zeros_like(drop_ref)
    qs = [q_ref[:, h * LANES:(h + 1) * LANES] for h in range(hp)]
    after = after_ref[...]

    def step(j, masked):
        ks = pl.ds(pl.multiple_of(j * tk, tk), tk)
        zs = [_dot_nt(qs[h], k_ref[ks, h * LANES:(h + 1) * LANES]) for h in range(hp)]
        for h in range(hp):
            z = zs[h]
            sp = jnp.maximum(z, 0.0) + jnp.log2(1.0 + jnp.exp2(_neg_abs(z)))
            if masked:
                row = lax.broadcasted_iota(jnp.int32, (tq, tk), 0)
                col = lax.broadcasted_iota(jnp.int32, (tq, tk), 1)
                strict = col < row
                sp = jnp.where(strict, sp, 0.0)
            drop = drop_ref[h]
            parts = [None] * (tk // sub)
            for b in reversed(range(tk // sub)):
                cs = slice(b * sub, (b + 1) * sub)
                sp_b = sp[:, cs]
                within = _dot(sp_b.astype(BF16), after)
                a_b = jnp.exp2(z[:, cs] - sp_b - within - _lane_tile(drop, sub))
                if masked:
                    a_b = jnp.where(strict[:, cs], a_b, 0.0)
                parts[b] = a_b.astype(BF16)
                drop = drop + jnp.sum(sp_b, axis=-1, keepdims=True)
            acc_ref[h] += _dot(jnp.concatenate(parts, axis=1), v_ref[ks, h * LANES:(h + 1) * LANES])
            drop_ref[h] = drop

    step(i, True)

    def live():
        return jnp.min(drop_ref[...]) < SB_DEAD_LOG2

    def body(carry):
        t, _ = carry
        step(i - 1 - t, False)
        return t + 1, live()

    lax.while_loop(lambda c: jnp.logical_and(c[0] < i, c[1]), body, (jnp.int32(0), live()))
    for h in range(hp):
        o_ref[:, h * LANES:(h + 1) * LANES] = acc_ref[h].astype(o_ref.dtype)


def _sb_attn(qkv, batch, seq_len, tq=256, sub=256, hp=8):
    T = qkv.shape[0]
    H, Dh = SB_HEADS, SB_HEAD_DIM
    assert Dh == LANES and H % hp == 0
    nq = seq_len // tq
    ng = H // hp
    after = (jnp.arange(sub)[:, None] > jnp.arange(sub)[None, :]).astype(BF16)
    return pl.pallas_call(
        functools.partial(_sb_attn_body, tk=tq, sub=sub, hp=hp),
        grid=(batch, ng, nq),
        in_specs=[
            pl.BlockSpec((tq, hp * Dh), lambda b, g, i: (b * nq + i, g)),
            pl.BlockSpec((seq_len, hp * Dh), lambda b, g, i: (b, ng + g)),
            pl.BlockSpec((seq_len, hp * Dh), lambda b, g, i: (b, 2 * ng + g)),
            pl.BlockSpec((sub, sub), lambda b, g, i: (0, 0)),
        ],
        out_specs=pl.BlockSpec((tq, hp * Dh), lambda b, g, i: (b * nq + i, g)),
        out_shape=jax.ShapeDtypeStruct((T, H * Dh), BF16),
        scratch_shapes=[pltpu.VMEM((hp, tq, Dh), F32), pltpu.VMEM((hp, tq, LANES), F32)],
        compiler_params=_cparams("parallel", "parallel", "arbitrary"),
        name="sb_attn",
    )(qkv, qkv, qkv, after)


def _sb_layer(x, norm_w, wqkv, wo, batch, seq_len, ffn_w_up, ffn_w_down, layer):
    n_q = SB_HEADS * SB_HEAD_DIM
    q_scale = SB_HEAD_DIM ** -0.5 * LOG2_E
    col_scale = jnp.where(jnp.arange(wqkv.shape[1]) < n_q, q_scale, 1.0).astype(F32)
    w = (wqkv * col_scale[None, :]).astype(BF16)
    qkv, ffn_up, ffn_down = _proj(
        x, w, norm_w=norm_w, out_dtype=BF16, name="sb_qkv",
        casts=(_cast_spec(ffn_w_up, layer, 1, 48), _cast_spec(ffn_w_down, layer, 0, 48)))
    o = _sb_attn(qkv, batch, seq_len)
    return _proj(o, wo.astype(BF16), res=x, name="sb_out"), ffn_up, ffn_down


def _ssd_body(zx_ref, dt_ref, cw_ref, dtb_ref, alog_ref, dfull_ref, nw_ref, exp_ref, o_ref,
              state_ref, carry_ref, buf_ref, y_ref, *, d_inner):
    c = pl.program_id(1)
    L = zx_ref.shape[0]
    G, N, P = SSD_GROUPS, SSD_STATE, SSD_HEAD_DIM
    gw = d_inner // G
    conv_ch = d_inner + 2 * G * N

    @pl.when(c == 0)
    def _():
        state_ref[...] = jnp.zeros_like(state_ref)
        carry_ref[...] = jnp.zeros_like(carry_ref)

    u = zx_ref[:, d_inner:d_inner + conv_ch]
    buf_ref[0:SUBLANES, :] = carry_ref[...]
    buf_ref[SUBLANES:SUBLANES + L, :] = u
    carry_ref[...] = u[L - SUBLANES:L, :]
    cw = cw_ref[...]
    conv = cw[SSD_CONV:SSD_CONV + 1, :] + cw[SSD_CONV - 1:SSD_CONV, :] * u
    win = buf_ref[...]
    for k in range(SSD_CONV - 1):
        conv = conv + cw[k:k + 1, :] * pltpu.roll(win, SSD_CONV - 1 - k, 0)[SUBLANES:, :]
    xbc = _silu(conv)
    xs = xbc[:, :d_inner]
    bm = xbc[:, d_inner:d_inner + G * N]
    cm = xbc[:, d_inner + G * N:]

    dt = _softplus(dt_ref[...] + dtb_ref[...])
    a = dt * (-jnp.exp(alog_ref[...]))
    r_i = lax.broadcasted_iota(jnp.int32, (L, L), 0)
    c_i = lax.broadcasted_iota(jnp.int32, (L, L), 1)
    causal = c_i <= r_i
    acum_t = _dot_split(a.T, jnp.where(c_i >= r_i, 1.0, 0.0).astype(BF16))
    acum = acum_t.T
    expand = exp_ref[...]
    dt_full = _dot_split(dt, expand)
    acum_full = _dot_split(acum, expand)
    last_full = acum_full[L - 1:L, :]
    xdt = xs * dt_full
    xdt_bf = xdt.astype(BF16)
    xdec_bf = (xdt * jnp.exp(last_full - acum_full)).astype(BF16)
    grow = jnp.exp(acum_full)
    chunk_decay = jnp.exp(last_full)
    lane = lax.broadcasted_iota(jnp.int32, (L, LANES), 1)
    first_half = lane < P

    for g in range(G):
        b_g = bm[:, g * N:(g + 1) * N]
        c_g = cm[:, g * N:(g + 1) * N].astype(BF16)
        cb = _dot_nt(c_g, b_g.astype(BF16))
        prev = state_ref[g]
        gs = slice(g * gw, (g + 1) * gw)
        y_off = _dot(c_g, prev.astype(BF16)) * grow[:, gs]
        parts = []
        for pair in range(gw // LANES):
            h0 = g * (gw // P) + 2 * pair
            xp = xdt_bf[:, h0 * P:h0 * P + LANES]
            ms = []
            for h in (h0, h0 + 1):
                seg = acum[:, h:h + 1] - acum_t[h:h + 1, :]
                ms.append((cb * jnp.exp(jnp.where(causal, seg, -jnp.inf))).astype(BF16))
            zero = jnp.zeros_like(xp)
            parts.append(_dot(ms[0], jnp.where(first_half, xp, zero)) + _dot(ms[1], jnp.where(first_half, zero, xp)))
        y_ref[:, gs] = jnp.concatenate(parts, axis=1) + y_off
        state_ref[g] = prev * chunk_decay[:, gs] + _dot(b_g.T.astype(BF16), xdec_bf[:, gs])

    y = (y_ref[...] + dfull_ref[...] * xs) * _silu(zx_ref[:, :d_inner])
    nw = nw_ref[...]
    for g in range(G):
        gs = slice(g * gw, (g + 1) * gw)
        o_ref[:, gs] = _rms_rows(y[:, gs], nw[:, gs]).astype(o_ref.dtype)


def _ssd_core(zx, dt_raw, cw, dt_bias, a_log, d_full, norm_w, expand, batch, seq_len, chunk=128):
    T = zx.shape[0]
    d_inner = d_full.shape[1]
    conv_ch = cw.shape[1]
    nc = seq_len // chunk
    G, N = SSD_GROUPS, SSD_STATE
    row = lambda n: pl.BlockSpec((1, n), lambda b, c: (0, 0))
    return pl.pallas_call(
        functools.partial(_ssd_body, d_inner=d_inner),
        grid=(batch, nc),
        in_specs=[
            pl.BlockSpec((chunk, zx.shape[1]), lambda b, c: (b * nc + c, 0)),
            pl.BlockSpec((chunk, LANES), lambda b, c: (b * nc + c, 0)),
            pl.BlockSpec((SUBLANES, conv_ch), lambda b, c: (0, 0)),
            row(LANES), row(LANES), row(d_inner), row(d_inner),
            pl.BlockSpec((LANES, d_inner), lambda b, c: (0, 0)),
        ],
        out_specs=pl.BlockSpec((chunk, d_inner), lambda b, c: (b * nc + c, 0)),
        out_shape=jax.ShapeDtypeStruct((T, d_inner), BF16),
        scratch_shapes=[
            pltpu.VMEM((G, N, d_inner // G), F32),
            pltpu.VMEM((SUBLANES, conv_ch), F32),
            pltpu.VMEM((SUBLANES + chunk, conv_ch), F32),
            pltpu.VMEM((chunk, d_inner), F32),
        ],
        compiler_params=_cparams("parallel", "arbitrary"),
        name="ssd_core",
    )(zx, dt_raw, cw, dt_bias, a_log, d_full, norm_w, expand)


def _ssd_layer(x, norm_w, w_in, conv_w, conv_b, dt_bias, a_log, d_skip, ssd_norm, w_out, batch, seq_len,
               ffn_w_up, ffn_w_down, layer):
    heads = d_skip.shape[0]
    d_inner = heads * SSD_HEAD_DIM
    n_zx = w_in.shape[1] - heads
    assert heads <= LANES
    pad = lambda v: jnp.concatenate([v.astype(F32), jnp.zeros((LANES - heads,), F32)]).reshape(1, LANES)
    w_dt = jnp.concatenate([w_in[:, n_zx:], jnp.zeros((w_in.shape[0], LANES - heads), F32)], axis=1).astype(BF16)
    dt_raw, h = _proj(x, w_dt, norm_w=norm_w, emit_norm=True, name="ssd_dt")
    zx, ffn_up = _proj(h, w_in.astype(BF16), cols=(0, n_zx), name="ssd_in",
                       casts=(_cast_spec(ffn_w_up, layer, 1, 64),))
    expand = (jnp.arange(LANES)[:, None] == (jnp.arange(d_inner) // SSD_HEAD_DIM)[None, :]).astype(BF16)
    d_full = jnp.repeat(d_skip.astype(F32), SSD_HEAD_DIM).reshape(1, d_inner)
    y = _ssd_core(zx, dt_raw, _conv_rows(conv_w, conv_b), pad(dt_bias), pad(a_log), d_full,
                  ssd_norm.reshape(1, d_inner).astype(F32), expand, batch, seq_len)
    x, ffn_down = _proj(y, w_out.astype(BF16), res=x, name="ssd_out", casts=(_cast_spec(ffn_w_down, layer, 0, 32),))
    return x, ffn_up, ffn_down


def _ret_body(qk_ref, v_ref, g_ref, o_ref, state_ref, decay_ref, *, heads, dk, dv):
    c = pl.program_id(1)
    L = qk_ref.shape[0]

    @pl.when(c == 0)
    def _():
        state_ref[...] = jnp.zeros_like(state_ref)
        r_i = lax.broadcasted_iota(jnp.int32, (L, L), 0)
        c_i = lax.broadcasted_iota(jnp.int32, (L, L), 1)
        diff = (r_i - c_i).astype(F32)
        for h in range(heads):
            log_g = math.log(1.0 - 2.0 ** (-5.0 - h))
            decay_ref[h] = jnp.where(diff >= 0, jnp.exp(jnp.maximum(diff, 0.0) * log_g), 0.0)

    idx = lax.broadcasted_iota(jnp.int32, (L, 1), 0).astype(F32)

    for h in range(heads):
        log_g = math.log(1.0 - 2.0 ** (-5.0 - h))
        d_intra = decay_ref[h]
        k_dec = jnp.exp((L - 1.0 - idx) * log_g)
        q_dec = jnp.exp((idx + 1.0) * log_g)
        q = qk_ref[:, h * dk:(h + 1) * dk]
        k = qk_ref[:, (heads + h) * dk:(heads + h + 1) * dk]
        vs = slice(h * dv, (h + 1) * dv)
        v = v_ref[:, vs]
        prev = state_ref[h]
        scores = (_dot_nt(q, k) * d_intra).astype(BF16)
        o = _dot(scores, v) + q_dec * _dot(q, prev.astype(BF16))
        kd_t = (k.astype(F32) * k_dec).T.astype(BF16)
        state_ref[h] = math.exp(L * log_g) * prev + _dot(kd_t, v)
        ms = jnp.mean(o * o, axis=-1, keepdims=True)
        o_ref[:, vs] = (_silu(g_ref[:, vs]) * (o * lax.rsqrt(ms + RMS_EPS))).astype(o_ref.dtype)


def _ret_core(qk, v, g, batch, seq_len, dk, dv, chunk=256):
    T = qk.shape[0]
    H = RET_HEADS
    nc = seq_len // chunk
    blk = lambda n: pl.BlockSpec((chunk, n), lambda b, c: (b * nc + c, 0))
    return pl.pallas_call(
        functools.partial(_ret_body, heads=H, dk=dk, dv=dv),
        grid=(batch, nc),
        in_specs=[blk(2 * H * dk), blk(H * dv), blk(H * dv)],
        out_specs=blk(H * dv),
        out_shape=jax.ShapeDtypeStruct((T, H * dv), BF16),
        scratch_shapes=[pltpu.VMEM((H, dk, dv), F32), pltpu.VMEM((H, chunk, chunk), F32)],
        compiler_params=_cparams("parallel", "arbitrary"),
        name="ret_core",
    )(qk, v, g)


def _ret_layer(x, norm_w, tabs, w_in, wo, batch, seq_len, ffn_w_up, ffn_w_down, layer):
    cos_r, sin_r = tabs[2], tabs[3]
    D = x.shape[1]
    H = RET_HEADS
    dk = D // H
    dv = 2 * D // H
    n_qk = 2 * H * dk
    n_v = H * dv
    n_all = w_in.shape[1]
    col = jnp.arange(n_all)
    col_scale = jnp.where((col >= H * dk) & (col < n_qk), dk ** -0.5, 1.0).astype(F32)
    w = (w_in * col_scale[None, :]).astype(BF16)
    g, h = _proj(x, w, cols=(n_qk + n_v, n_all - n_qk - n_v), norm_w=norm_w, emit_norm=True, name="ret_g")
    qk, ffn_up = _proj(h, w, cols=(0, n_qk), epi="rope_half", cos=cos_r, sin=sin_r, out_dtype=BF16,
                       casts=(_cast_spec(ffn_w_up, layer, 1, 32),), name="ret_qk")
    v, ffn_down = _proj(h, w, cols=(n_qk, n_v), out_dtype=BF16,
                        casts=(_cast_spec(ffn_w_down, layer, 0, 32),), name="ret_v")
    o = _ret_core(qk, v, g, batch, seq_len, dk, dv)
    return _proj(o, wo.astype(BF16), res=x, name="ret_out"), ffn_up, ffn_down


def kernel(x, positions, norm_mix, norm_ffn, norm_final, mla_wq_a, mla_q_norm, mla_wq_b, mla_wkv_a, mla_kv_norm, mla_wkv_b, mla_wo, sb_wqkv, sb_wo, ssd_w_in, ssd_conv_w, ssd_conv_b, ssd_dt_bias, ssd_a_log, ssd_d, ssd_norm, ssd_w_out, ret_w_in, ret_wo, ffn_w_up, ffn_conv_w, ffn_conv_b, ffn_w_down):
    B, S, D = x.shape
    depth = norm_mix.shape[0]
    n_mixers = 4
    tabs = _rope_tables(positions, D // RET_HEADS)
    cw = _conv_rows(ffn_conv_w, ffn_conv_b)
    xt = x.reshape(B * S, D)
    for i in range(depth):
        m, j = i % n_mixers, i // n_mixers
        ffn = (ffn_w_up, ffn_w_down, i)
        if m == 0:
            xt, w_up, w_down = _mla_layer(xt, norm_mix[i], tabs, mla_wq_a[j], mla_q_norm[j], mla_wq_b[j],
                                          mla_wkv_a[j], mla_kv_norm[j], mla_wkv_b[j], mla_wo[j], B, S, *ffn)
        elif m == 1:
            xt, w_up, w_down = _sb_layer(xt, norm_mix[i], sb_wqkv[j], sb_wo[j], B, S, *ffn)
        elif m == 2:
            xt, w_up, w_down = _ssd_layer(xt, norm_mix[i], ssd_w_in[j], ssd_conv_w[j], ssd_conv_b[j],
                                          ssd_dt_bias[j], ssd_a_log[j], ssd_d[j], ssd_norm[j], ssd_w_out[j],
                                          B, S, *ffn)
        else:
            xt, w_up, w_down = _ret_layer(xt, norm_mix[i], tabs, ret_w_in[j], ret_wo[j], B, S, *ffn)
        xt = _ffn(xt, norm_ffn[i], w_up, cw, w_down, i, S)
    return _rmsnorm(xt, norm_final).reshape(B, S, D)
```

```python
import functools
import math

import jax
import jax.numpy as jnp
from jax import lax
from jax.experimental import pallas as pl
from jax.experimental.pallas import tpu as pltpu

F32 = jnp.float32
BF16 = jnp.bfloat16

RMS_EPS = 1e-6
ROPE_BASE = 10000.0
LOG2_E = 1.4426950408889634
SB_DEAD_LOG2 = 160.0

MLA_HEADS = 16
MLA_Q_RANK = 512
MLA_KV_RANK = 512
MLA_NOPE = 128
MLA_ROPE = 64
MLA_V = 128
SB_HEADS = 16
SB_HEAD_DIM = 128
SSD_HEAD_DIM = 64
SSD_GROUPS = 8
SSD_STATE = 128
SSD_CONV = 4
RET_HEADS = 8
FFN_CONV = 3

LANES = 128
SUBLANES = 8
VMEM_LIMIT_BYTES = 56 * 1024 * 1024
PROJ_VMEM_BUDGET_BYTES = 48 * 1024 * 1024


def _cparams(*sem):
    return pltpu.CompilerParams(dimension_semantics=sem, vmem_limit_bytes=VMEM_LIMIT_BYTES)


def _dot(a, b):
    return jnp.dot(a, b, preferred_element_type=F32)


def _dot_nt(a, b):
    return lax.dot_general(a, b, (((1,), (1,)), ((), ())), preferred_element_type=F32)


def _dot_split(a, b):
    hi = a.astype(BF16)
    lo = (a - hi.astype(F32)).astype(BF16)
    return _dot(hi, b) + _dot(lo, b)


def _rms_rows(xf, w):
    ms = jnp.mean(xf * xf, axis=-1, keepdims=True)
    return xf * lax.rsqrt(ms + RMS_EPS) * w


def _silu(x):
    return x * jax.nn.sigmoid(x)


def _neg_abs(x):
    bits = lax.bitcast_convert_type(x, jnp.uint32) | jnp.uint32(0x80000000)
    return lax.bitcast_convert_type(bits, F32)


def _softplus(x):
    return jnp.maximum(x, 0.0) + jnp.log(1.0 + jnp.exp(-jnp.abs(x)))


def _rope_tables_body(pos_ref, inv_a_ref, sgn_a_ref, inv_r_ref, cos_a_ref, sin_a_ref, cos_r_ref, sin_r_ref):
    pos = pos_ref[...]
    ang_a = pos * inv_a_ref[...]
    cos_a_ref[...] = jnp.cos(ang_a)
    sin_a_ref[...] = jnp.sin(ang_a) * sgn_a_ref[...]
    ang_r = pos * inv_r_ref[...]
    cos_r_ref[...] = jnp.cos(ang_r)
    sin_r_ref[...] = jnp.sin(ang_r)


def _rope_tables(positions, ret_dk):
    T = positions.size
    tm = 1024
    pos = positions.reshape(T, 1).astype(F32)
    half_a = MLA_ROPE // 2
    inv_a = 1.0 / (ROPE_BASE ** (jnp.arange(half_a, dtype=F32) * (2.0 / MLA_ROPE)))
    z = jnp.zeros((half_a,), F32)
    inv_a = jnp.concatenate([inv_a, z, inv_a, z]).reshape(1, LANES)
    o = jnp.ones((2 * half_a,), F32)
    sgn_a = jnp.concatenate([-o, o]).reshape(1, LANES)
    half_r = ret_dk // 2
    assert half_r == LANES
    inv_r = (1.0 / (ROPE_BASE ** (jnp.arange(half_r, dtype=F32) * (2.0 / ret_dk)))).reshape(1, LANES)
    row = pl.BlockSpec((1, LANES), lambda i: (0, 0))
    tab = pl.BlockSpec((tm, LANES), lambda i: (i, 0))
    shp = jax.ShapeDtypeStruct((T, LANES), F32)
    return pl.pallas_call(
        _rope_tables_body,
        grid=(T // tm,),
        in_specs=[pl.BlockSpec((tm, 1), lambda i: (i, 0)), row, row, row],
        out_specs=[tab, tab, tab, tab],
        out_shape=[shp, shp, shp, shp],
        compiler_params=_cparams("parallel"),
        name="rope_tables",
    )(pos, inv_a, sgn_a, inv_r)


def _proj_body(*refs, has_norm, epi, has_res, n_casts):
    it = iter(refs)
    a_ref = next(it)
    nw_ref = next(it) if has_norm else None
    w_ref = next(it)
    res_ref = next(it) if has_res else None
    cos_ref = sin_ref = None
    if epi is not None:
        cos_ref = next(it)
        sin_ref = next(it)
    cast_src = [next(it) for _ in range(n_casts)]
    o_ref = next(it)
    cast_dst = [next(it) for _ in range(n_casts)]
    abf_ref = next(it) if has_norm else None

    for src_ref, dst_ref in zip(cast_src, cast_dst):
        dst_ref[...] = src_ref[...].astype(BF16)

    if has_norm:
        @pl.when(pl.program_id(1) == 0)
        def _():
            abf_ref[...] = _rms_rows(a_ref[...].astype(F32), nw_ref[...]).astype(BF16)
        a = abf_ref[...]
    else:
        a = a_ref[...]
    acc = _dot(a, w_ref[...])
    if epi == "rope_half":
        c = cos_ref[...]
        s = sin_ref[...]
        parts = []
        for j in range(acc.shape[1] // (2 * LANES)):
            x1 = acc[:, (2 * j) * LANES:(2 * j + 1) * LANES]
            x2 = acc[:, (2 * j + 1) * LANES:(2 * j + 2) * LANES]
            parts += [x1 * c - x2 * s, x2 * c + x1 * s]
        acc = jnp.concatenate(parts, axis=1)
    elif epi == "rope_group":
        c = cos_ref[...]
        s = sin_ref[...]
        parts = []
        for j in range(acc.shape[1] // (2 * LANES)):
            g = acc[:, (2 * j + 1) * LANES:(2 * j + 2) * LANES]
            parts += [acc[:, (2 * j) * LANES:(2 * j + 1) * LANES], g * c + pltpu.roll(g, LANES // 2, 1) * s]
        acc = jnp.concatenate(parts, axis=1)
    if has_res:
        acc = acc + res_ref[...]
    o_ref[...] = acc.astype(o_ref.dtype)


def _proj_tiles(T, K, N, a_bytes, out_bytes, norm_bufs, has_res, side_bytes, min_steps):
    for tm in (1024, 512, 256):
        if T % tm:
            continue
        for tn in (1024, 512, 256, LANES):
            if N % tn or (T // tm) * (N // tn) < min_steps:
                continue
            need = (2 * tm * K * a_bytes + norm_bufs * tm * K * 2 + 2 * K * tn * 2 + side_bytes
                    + 2 * tm * tn * out_bytes + (2 * tm * tn * 4 if has_res else 0) + tm * tn * 4)
            if need <= PROJ_VMEM_BUDGET_BYTES:
                return tm, tn
    raise ValueError("no projection tiling fits VMEM")


def _proj(a, w, *, cols=None, norm_w=None, res=None, epi=None, cos=None, sin=None, out_dtype=F32, casts=(),
          emit_norm=False, name="proj"):
    T, K = a.shape
    c0, N = (0, w.shape[1]) if cols is None else cols
    has_norm = norm_w is not None
    has_res = res is not None
    assert has_norm or not emit_norm
    side_bytes = sum(2 * br * bc * (4 + 2) for _, _, (br, bc) in casts)
    min_steps = max([(s.shape[1] // br) * (s.shape[2] // bc) for s, _, (br, bc) in casts], default=1)
    tm, tn = _proj_tiles(T, K, N, a.dtype.itemsize, jnp.dtype(out_dtype).itemsize, has_norm + emit_norm, has_res,
                         side_bytes, min_steps)
    assert c0 % tn == 0
    nb0 = c0 // tn
    in_specs = [pl.BlockSpec((tm, K), lambda m, n: (m, 0))]
    args = [a]
    if has_norm:
        in_specs.append(pl.BlockSpec((1, K), lambda m, n: (0, 0)))
        args.append(norm_w.reshape(1, K).astype(F32))
    in_specs.append(pl.BlockSpec((K, tn), lambda m, n: (0, nb0 + n)))
    args.append(w)
    if has_res:
        in_specs.append(pl.BlockSpec((tm, tn), lambda m, n: (m, n)))
        args.append(res)
    if epi is not None:
        in_specs += [pl.BlockSpec((tm, LANES), lambda m, n: (m, 0))] * 2
        args += [cos, sin]
    scratch = [pltpu.VMEM((tm, K), BF16)] if has_norm and not emit_norm else []
    out_specs = [pl.BlockSpec((tm, tn), lambda m, n: (m, n))]
    out_shape = [jax.ShapeDtypeStruct((T, N), out_dtype)]
    n_n = N // tn
    n_steps = (T // tm) * n_n
    for src, layer, (br, bc) in casts:
        _, R, C = src.shape
        assert R % br == 0 and C % bc == 0 and (R // br) * (C // bc) <= n_steps
        ncc, last = C // bc, (R // br) * (C // bc) - 1

        def chunk(m, n, ncc=ncc, last=last):
            c = jnp.minimum(m * n_n + n, last)
            return c // ncc, c % ncc

        in_specs.append(pl.BlockSpec((None, br, bc), lambda m, n, layer=layer, chunk=chunk: (layer, *chunk(m, n))))
        args.append(src)
        out_specs.append(pl.BlockSpec((br, bc), chunk))
        out_shape.append(jax.ShapeDtypeStruct((R, C), BF16))
    if emit_norm:
        out_specs.append(pl.BlockSpec((tm, K), lambda m, n: (m, 0)))
        out_shape.append(jax.ShapeDtypeStruct((T, K), BF16))
    outs = pl.pallas_call(
        functools.partial(_proj_body, has_norm=has_norm, epi=epi, has_res=has_res, n_casts=len(casts)),
        grid=(T // tm, n_n),
        in_specs=in_specs,
        out_specs=out_specs,
        out_shape=out_shape,
        scratch_shapes=scratch,
        compiler_params=_cparams("arbitrary", "arbitrary"),
        name=name,
    )(*args)
    return outs if casts or emit_norm else outs[0]


def _rmsnorm_body(x_ref, w_ref, o_ref):
    o_ref[...] = _rms_rows(x_ref[...], w_ref[...])


def _rmsnorm(x, w, tm=512):
    T, D = x.shape
    return pl.pallas_call(
        _rmsnorm_body,
        grid=(T // tm,),
        in_specs=[pl.BlockSpec((tm, D), lambda m: (m, 0)), pl.BlockSpec((1, D), lambda m: (0, 0))],
        out_specs=pl.BlockSpec((tm, D), lambda m: (m, 0)),
        out_shape=jax.ShapeDtypeStruct((T, D), F32),
        compiler_params=_cparams("parallel"),
        name="final_norm",
    )(x, w.reshape(1, D).astype(F32))


def _ffn_body(x_ref, nw_ref, wg_ref, wu_ref, cg_ref, cu_ref, cgp_ref, cup_ref, wd_ref, xres_ref, o_ref,
              h_ref, act_ref, ag_ref, au_ref, bg_ref, bu_ref, carry_g_ref, carry_u_ref, *, tiles_per_seq, nf):
    m = pl.program_id(0)
    f = pl.program_id(1)
    tm = x_ref.shape[0]
    tf = wg_ref.shape[1]
    half = tm // 2

    def activation(g_ref, u_ref, cwg, cwu):
        def conv(buf_ref, cw):
            win = buf_ref[...]
            return (cw[0:1, :] * pltpu.roll(win, 2, 0)[SUBLANES:, :]
                    + cw[1:2, :] * pltpu.roll(win, 1, 0)[SUBLANES:, :]
                    + cw[2:3, :] * win[SUBLANES:, :] + cw[3:4, :])
        return (_silu(conv(g_ref, cwg)) * conv(u_ref, cwu)).astype(BF16)

    def act_cols(fi):
        return pl.ds(pl.multiple_of(fi * tf, tf), tf)

    @pl.when(f == 0)
    def _():
        h_ref[...] = _rms_rows(x_ref[...], nw_ref[...]).astype(BF16)

    @pl.when(jnp.logical_and(f == 0, m == 0))
    def _():
        carry_g_ref[...] = jnp.zeros_like(carry_g_ref)
        carry_u_ref[...] = jnp.zeros_like(carry_u_ref)
        bg_ref[...] = jnp.zeros_like(bg_ref)
        bu_ref[...] = jnp.zeros_like(bu_ref)

    def finish_previous(fi):
        act_ref[0:half, act_cols(fi)] = activation(ag_ref, au_ref, cgp_ref[...], cup_ref[...])
        act_ref[half:, act_cols(fi)] = activation(bg_ref, bu_ref, cgp_ref[...], cup_ref[...])

    @pl.when(f < nf)
    def _up():
        finish_previous(jnp.maximum(f - 1, 0))
        seq_start = m % tiles_per_seq == 0
        ag_ref[0:SUBLANES, :] = jnp.where(seq_start, 0.0, carry_g_ref[f])
        au_ref[0:SUBLANES, :] = jnp.where(seq_start, 0.0, carry_u_ref[f])
        h_a = h_ref[0:half, :]
        ag_ref[SUBLANES:, :] = _dot(h_a, wg_ref[...])
        au_ref[SUBLANES:, :] = _dot(h_a, wu_ref[...])
        h_b = h_ref[half:, :]
        bg_ref[0:SUBLANES, :] = ag_ref[half:half + SUBLANES, :]
        bu_ref[0:SUBLANES, :] = au_ref[half:half + SUBLANES, :]
        bg_ref[SUBLANES:, :] = _dot(h_b, wg_ref[...])
        bu_ref[SUBLANES:, :] = _dot(h_b, wu_ref[...])
        carry_g_ref[f] = bg_ref[half:half + SUBLANES, :]
        carry_u_ref[f] = bu_ref[half:half + SUBLANES, :]

    @pl.when(f == nf)
    def _():
        finish_previous(nf - 1)

    @pl.when(f >= nf)
    def _down():
        o_ref[...] = xres_ref[...] + _dot(act_ref[...], wd_ref[...])


def _ffn(x, norm_w, w_up, cw, w_down, layer, seq_len, tm=1024, tf=512, td=256):
    T, D = x.shape
    F = w_down.shape[0]
    assert T % tm == 0 and F % tf == 0 and seq_len % tm == 0 and D % td == 0 and (tm // 2) % SUBLANES == 0
    nf = F // tf
    nd = D // td
    cur = lambda f: jnp.minimum(f, nf - 1)
    prev = lambda f: jnp.maximum(jnp.minimum(f, nf) - 1, 0)
    gate_blk = lambda m, f: (0, cur(f))
    up_blk = lambda m, f: (0, nf + cur(f))
    cgate_blk = lambda m, f: (layer, 0, cur(f))
    cup_blk = lambda m, f: (layer, 0, nf + cur(f))
    gate_prev = lambda m, f: (layer, 0, prev(f))
    up_prev = lambda m, f: (layer, 0, nf + prev(f))
    down_blk = lambda m, f: (0, jnp.maximum(f - nf, 0))
    out_blk = lambda m, f: (m, jnp.maximum(f - nf, 0))
    stage = pltpu.VMEM((SUBLANES + tm // 2, tf), F32)
    return pl.pallas_call(
        functools.partial(_ffn_body, tiles_per_seq=seq_len // tm, nf=nf),
        grid=(T // tm, nf + nd),
        in_specs=[
            pl.BlockSpec((tm, D), lambda m, f: (m, 0)),
            pl.BlockSpec((1, D), lambda m, f: (0, 0)),
            pl.BlockSpec((D, tf), gate_blk),
            pl.BlockSpec((D, tf), up_blk),
            pl.BlockSpec((None, SUBLANES, tf), cgate_blk),
            pl.BlockSpec((None, SUBLANES, tf), cup_blk),
            pl.BlockSpec((None, SUBLANES, tf), gate_prev),
            pl.BlockSpec((None, SUBLANES, tf), up_prev),
            pl.BlockSpec((F, td), down_blk),
            pl.BlockSpec((tm, td), out_blk),
        ],
        out_specs=pl.BlockSpec((tm, td), out_blk),
        out_shape=jax.ShapeDtypeStruct((T, D), F32),
        scratch_shapes=[
            pltpu.VMEM((tm, D), BF16),
            pltpu.VMEM((tm, F), BF16),
            stage, stage, stage, stage,
            pltpu.VMEM((nf, SUBLANES, tf), F32),
            pltpu.VMEM((nf, SUBLANES, tf), F32),
        ],
        compiler_params=_cparams("arbitrary", "arbitrary"),
        name="conv_ffn",
    )(x, norm_w.reshape(1, D).astype(F32), w_up, w_up, cw, cw, cw, cw, w_down, x)


def _cast_spec(src, layer, axis, max_chunks):
    shape = src.shape[1:]
    for width in (LANES, 2 * LANES, 4 * LANES, 8 * LANES):
        if shape[axis] % width == 0 and shape[axis] // width <= max_chunks:
            return src, layer, ((width, shape[1]) if axis == 0 else (shape[0], width))
    raise ValueError("no chunking of the cast source fits the host's grid")


def _conv_rows(w, b):
    K, C = w.shape[-2:]
    pad = jnp.zeros(w.shape[:-2] + (SUBLANES - K - 1, C), F32)
    return jnp.concatenate([w.astype(F32), b.astype(F32)[..., None, :], pad], axis=-2)


def _mla_down_body(x_ref, nw_ref, w_ref, qn_ref, kvn_ref, cos_ref, sin_ref, cq_ref, ckv_ref, kpe_ref):
    h = _rms_rows(x_ref[...], nw_ref[...]).astype(BF16)
    r = _dot(h, w_ref[...])
    cq_ref[...] = _rms_rows(r[:, :MLA_Q_RANK], qn_ref[...]).astype(BF16)
    ckv_ref[...] = _rms_rows(r[:, MLA_Q_RANK:MLA_Q_RANK + MLA_KV_RANK], kvn_ref[...]).astype(BF16)
    g = r[:, MLA_Q_RANK + MLA_KV_RANK:]
    kpe_ref[...] = (g * cos_ref[...] + pltpu.roll(g, LANES // 2, 1) * sin_ref[...]).astype(BF16)


def _mla_down(x, norm_w, w, q_norm, kv_norm, cos, sin, tm=512):
    T, D = x.shape
    N = w.shape[1]
    row = lambda n: pl.BlockSpec((1, n), lambda m: (0, 0))
    tile = lambda n: pl.BlockSpec((tm, n), lambda m: (m, 0))
    return pl.pallas_call(
        _mla_down_body,
        grid=(T // tm,),
        in_specs=[tile(D), row(D), pl.BlockSpec((D, N), lambda m: (0, 0)), row(MLA_Q_RANK), row(MLA_KV_RANK),
                  tile(LANES), tile(LANES)],
        out_specs=[tile(MLA_Q_RANK), tile(MLA_KV_RANK), tile(LANES)],
        out_shape=[jax.ShapeDtypeStruct((T, MLA_Q_RANK), BF16), jax.ShapeDtypeStruct((T, MLA_KV_RANK), BF16),
                   jax.ShapeDtypeStruct((T, LANES), BF16)],
        compiler_params=_cparams("parallel"),
        name="mla_down",
    )(x, norm_w.reshape(1, D).astype(F32), w, q_norm.reshape(1, -1).astype(F32),
      kv_norm.reshape(1, -1).astype(F32), cos, sin)


def _mla_up_body(cq_ref, ckv_ref, w_ref, cos_ref, sin_ref, up_ref, down_ref, o_ref, upbf_ref, downbf_ref, *, n_q):
    n = pl.program_id(1)
    upbf_ref[...] = up_ref[...].astype(BF16)
    downbf_ref[...] = down_ref[...].astype(BF16)
    is_q = n < n_q
    acc = _dot(jnp.where(is_q, cq_ref[...], ckv_ref[...]), w_ref[...])
    c = cos_ref[...]
    s = sin_ref[...]
    parts = []
    for j in range(acc.shape[1] // (2 * LANES)):
        g = acc[:, (2 * j + 1) * LANES:(2 * j + 2) * LANES]
        parts += [acc[:, (2 * j) * LANES:(2 * j + 1) * LANES], g * c + pltpu.roll(g, LANES // 2, 1) * s]
    o_ref[...] = jnp.where(is_q, jnp.concatenate(parts, axis=1), acc).astype(o_ref.dtype)


def _mla_up(cq, ckv, wq, wkv, cos, sin, ffn_w_up, ffn_w_down, layer, tm=1024, tn=1024):
    T, R = cq.shape
    n_q, n_kv = wq.shape[1] // tn, wkv.shape[1] // tn
    n_n = n_q + n_kv
    w = jnp.concatenate([wq, wkv], axis=1)
    _, D, F2 = ffn_w_up.shape
    _, F, _ = ffn_w_down.shape
    bc, br = 2 * LANES, LANES
    n_up, n_down = F2 // bc, F // br
    assert T % tm == 0 and max(n_up, n_down) <= (T // tm) * n_n
    step = lambda m, n: m * n_n + n
    out, up_bf, down_bf = pl.pallas_call(
        functools.partial(_mla_up_body, n_q=n_q),
        grid=(T // tm, n_n),
        in_specs=[
            pl.BlockSpec((tm, R), lambda m, n: (m, 0)),
            pl.BlockSpec((tm, R), lambda m, n: (m, 0)),
            pl.BlockSpec((R, tn), lambda m, n: (0, n)),
            pl.BlockSpec((tm, LANES), lambda m, n: (m, 0)),
            pl.BlockSpec((tm, LANES), lambda m, n: (m, 0)),
            pl.BlockSpec((None, D, bc), lambda m, n: (layer, 0, jnp.minimum(step(m, n), n_up - 1))),
            pl.BlockSpec((None, br, D), lambda m, n: (layer, jnp.minimum(step(m, n), n_down - 1), 0)),
        ],
        out_specs=[
            pl.BlockSpec((tm, tn), lambda m, n: (m, n)),
            pl.BlockSpec((D, bc), lambda m, n: (0, jnp.minimum(step(m, n), n_up - 1))),
            pl.BlockSpec((br, D), lambda m, n: (jnp.minimum(step(m, n), n_down - 1), 0)),
        ],
        out_shape=[jax.ShapeDtypeStruct((T, n_n * tn), BF16), jax.ShapeDtypeStruct((D, F2), BF16),
                   jax.ShapeDtypeStruct((F, D), BF16)],
        compiler_params=_cparams("arbitrary", "arbitrary"),
        name="mla_up",
    )(cq, ckv, w, cos, sin, ffn_w_up, ffn_w_down)
    return out, up_bf, down_bf


def _lane_tile(x, width):
    return jnp.concatenate([x] * (width // LANES), axis=1)


def _mla_attn_body(q_ref, kn_ref, kpe_ref, v_ref, o_ref, kcat_ref, m_ref, l_ref, acc_ref, *, tk, hp):
    i = pl.program_id(2)
    tq = q_ref.shape[0]

    @pl.when(i == 0)
    def _():
        for h in range(hp):
            kcat_ref[h, :, :LANES] = kn_ref[:, h * LANES:(h + 1) * LANES]
            kcat_ref[h, :, LANES:] = kpe_ref[...]

    m_ref[...] = jnp.full(m_ref.shape, -jnp.inf, F32)
    l_ref[...] = jnp.zeros_like(l_ref)
    acc_ref[...] = jnp.zeros_like(acc_ref)
    qs = [q_ref[:, h * 2 * LANES:(h + 1) * 2 * LANES] for h in range(hp)]

    def step(j, masked):
        ks = pl.ds(pl.multiple_of(j * tk, tk), tk)
        ss = [_dot_nt(qs[h], kcat_ref[h, ks, :]) for h in range(hp)]
        for h in range(hp):
            s = ss[h]
            if masked:
                row = lax.broadcasted_iota(jnp.int32, (tq, tk), 0)
                col = lax.broadcasted_iota(jnp.int32, (tq, tk), 1)
                s = jnp.where(col <= row, s, -jnp.inf)
            m_prev = m_ref[h]
            m_new = jnp.maximum(m_prev, jnp.max(s, axis=-1, keepdims=True))
            alpha = jnp.exp2(m_prev - m_new)
            p = jnp.exp2(s - _lane_tile(m_new, tk))
            l_ref[h] = alpha * l_ref[h] + jnp.sum(p, axis=-1, keepdims=True)
            acc_ref[h] = alpha * acc_ref[h] + _dot(p.astype(BF16), v_ref[ks, h * LANES:(h + 1) * LANES])
            m_ref[h] = m_new

    def body(j, c):
        step(j, False)
        return c

    lax.fori_loop(0, i, body, 0)
    step(i, True)
    for h in range(hp):
        o_ref[:, h * LANES:(h + 1) * LANES] = (acc_ref[h] / l_ref[h]).astype(o_ref.dtype)


def _mla_attn(q, kv, kpe, batch, seq_len, tq=512, hp=4, kv_col0=0):
    T = q.shape[0]
    H = MLA_HEADS
    assert MLA_V == LANES and MLA_NOPE == LANES and H % hp == 0 and kv_col0 % (hp * LANES) == 0
    nq = seq_len // tq
    ng = H // hp
    kb0 = kv_col0 // (hp * LANES)
    return pl.pallas_call(
        functools.partial(_mla_attn_body, tk=tq, hp=hp),
        grid=(batch, ng, nq),
        in_specs=[
            pl.BlockSpec((tq, hp * 2 * LANES), lambda b, g, i: (b * nq + i, g)),
            pl.BlockSpec((seq_len, hp * LANES), lambda b, g, i: (b, kb0 + g)),
            pl.BlockSpec((seq_len, LANES), lambda b, g, i: (b, 0)),
            pl.BlockSpec((seq_len, hp * LANES), lambda b, g, i: (b, kb0 + ng + g)),
        ],
        out_specs=pl.BlockSpec((tq, hp * LANES), lambda b, g, i: (b * nq + i, g)),
        out_shape=jax.ShapeDtypeStruct((T, H * MLA_V), BF16),
        scratch_shapes=[
            pltpu.VMEM((hp, seq_len, 2 * LANES), BF16),
            pltpu.VMEM((hp, tq, LANES), F32),
            pltpu.VMEM((hp, tq, LANES), F32),
            pltpu.VMEM((hp, tq, LANES), F32),
        ],
        compiler_params=_cparams("parallel", "parallel", "arbitrary"),
        name="mla_attn",
    )(q, kv, kpe, kv)


def _mla_layer(x, norm_w, tabs, wq_a, q_norm, wq_b, wkv_a, kv_norm, wkv_b, wo, batch, seq_len,
               ffn_w_up, ffn_w_down, layer):
    cos_a, sin_a = tabs[0], tabs[1]
    D = x.shape[1]
    H = MLA_HEADS
    half = MLA_ROPE // 2
    scale = (MLA_NOPE + MLA_ROPE) ** -0.5 * LOG2_E
    zk = jnp.zeros((D, half), F32)
    w_down = jnp.concatenate(
        [wq_a, wkv_a[:, :MLA_KV_RANK], wkv_a[:, MLA_KV_RANK:MLA_KV_RANK + half], zk,
         wkv_a[:, MLA_KV_RANK + half:], zk], axis=1).astype(BF16)
    wq = wq_b.reshape(MLA_Q_RANK, H, MLA_NOPE + MLA_ROPE)
    zq = jnp.zeros((MLA_Q_RANK, H, half), F32)
    wq = jnp.concatenate([wq[:, :, :MLA_NOPE], wq[:, :, MLA_NOPE:MLA_NOPE + half], zq,
                          wq[:, :, MLA_NOPE + half:], zq], axis=2)
    wq = (wq * scale).reshape(MLA_Q_RANK, H * 2 * LANES).astype(BF16)
    wkv = wkv_b.reshape(MLA_KV_RANK, H, 2, MLA_NOPE).transpose(0, 2, 1, 3).reshape(MLA_KV_RANK, 2 * H * MLA_NOPE)
    wkv = wkv.astype(BF16)

    cq, ckv, kpe = _mla_down(x, norm_w, w_down, q_norm, kv_norm, cos_a, sin_a)
    qkv, ffn_up, ffn_down = _mla_up(cq, ckv, wq, wkv, cos_a, sin_a, ffn_w_up, ffn_w_down, layer)
    o = _mla_attn(qkv, qkv, kpe, batch, seq_len, kv_col0=wq.shape[1])
    return _proj(o, wo.astype(BF16), res=x, name="mla_out"), ffn_up, ffn_down


def _sb_attn_body(q_ref, k_ref, v_ref, after_ref, o_ref, acc_ref, drop_ref, *, tk, sub, hp):
    i = pl.program_id(2)
    tq = q_ref.shape[0]
    acc_ref[...] = jnp.zeros_like(acc_ref)
    drop_ref[...] = jnp.zeros_like(drop_ref)
    qs = [q_ref[:, h * LANES:(h + 1) * LANES] for h in range(hp)]
    after = after_ref[...]

    def step(j, masked):
        ks = pl.ds(pl.multiple_of(j * tk, tk), tk)
        zs = [_dot_nt(qs[h], k_ref[ks, h * LANES:(h + 1) * LANES]) for h in range(hp)]
        for h in range(hp):
            z = zs[h]
            sp = jnp.maximum(z, 0.0) + jnp.log2(1.0 + jnp.exp2(_neg_abs(z)))
            if masked:
                row = lax.broadcasted_iota(jnp.int32, (tq, tk), 0)
                col = lax.broadcasted_iota(jnp.int32, (tq, tk), 1)
                strict = col < row
                sp = jnp.where(strict, sp, 0.0)
            drop = drop_ref[h]
            parts = [None] * (tk // sub)
            for b in reversed(range(tk // sub)):
                cs = slice(b * sub, (b + 1) * sub)
                sp_b = sp[:, cs]
                within = _dot(sp_b.astype(BF16), after)
                a_b = jnp.exp2(z[:, cs] - sp_b - within - _lane_tile(drop, sub))
                if masked:
                    a_b = jnp.where(strict[:, cs], a_b, 0.0)
                parts[b] = a_b.astype(BF16)
                drop = drop + jnp.sum(sp_b, axis=-1, keepdims=True)
            acc_ref[h] += _dot(jnp.concatenate(parts, axis=1), v_ref[ks, h * LANES:(h + 1) * LANES])
            drop_ref[h] = drop

    step(i, True)

    def live():
        return jnp.min(drop_ref[...]) < SB_DEAD_LOG2

    def body(carry):
        t, _ = carry
        step(i - 1 - t, False)
        return t + 1, live()

    lax.while_loop(lambda c: jnp.logical_and(c[0] < i, c[1]), body, (jnp.int32(0), live()))
    for h in range(hp):
        o_ref[:, h * LANES:(h + 1) * LANES] = acc_ref[h].astype(o_ref.dtype)


def _sb_attn(qkv, batch, seq_len, tq=256, sub=256, hp=8):
    T = qkv.shape[0]
    H, Dh = SB_HEADS, SB_HEAD_DIM
    assert Dh == LANES and H % hp == 0
    nq = seq_len // tq
    ng = H // hp
    after = (jnp.arange(sub)[:, None] > jnp.arange(sub)[None, :]).astype(BF16)
    return pl.pallas_call(
        functools.partial(_sb_attn_body, tk=tq, sub=sub, hp=hp),
        grid=(batch, ng, nq),
        in_specs=[
            pl.BlockSpec((tq, hp * Dh), lambda b, g, i: (b * nq + i, g)),
            pl.BlockSpec((seq_len, hp * Dh), lambda b, g, i: (b, ng + g)),
            pl.BlockSpec((seq_len, hp * Dh), lambda b, g, i: (b, 2 * ng + g)),
            pl.BlockSpec((sub, sub), lambda b, g, i: (0, 0)),
        ],
        out_specs=pl.BlockSpec((tq, hp * Dh), lambda b, g, i: (b * nq + i, g)),
        out_shape=jax.ShapeDtypeStruct((T, H * Dh), BF16),
        scratch_shapes=[pltpu.VMEM((hp, tq, Dh), F32), pltpu.VMEM((hp, tq, LANES), F32)],
        compiler_params=_cparams("parallel", "parallel", "arbitrary"),
        name="sb_attn",
    )(qkv, qkv, qkv, after)


def _sb_layer(x, norm_w, wqkv, wo, batch, seq_len, ffn_w_up, ffn_w_down, layer):
    n_q = SB_HEADS * SB_HEAD_DIM
    q_scale = SB_HEAD_DIM ** -0.5 * LOG2_E
    col_scale = jnp.where(jnp.arange(wqkv.shape[1]) < n_q, q_scale, 1.0).astype(F32)
    w = (wqkv * col_scale[None, :]).astype(BF16)
    qkv, ffn_up, ffn_down = _proj(
        x, w, norm_w=norm_w, out_dtype=BF16, name="sb_qkv",
        casts=(_cast_spec(ffn_w_up, layer, 1, 48), _cast_spec(ffn_w_down, layer, 0, 48)))
    o = _sb_attn(qkv, batch, seq_len)
    return _proj(o, wo.astype(BF16), res=x, name="sb_out"), ffn_up, ffn_down


def _ssd_body(zx_ref, dt_ref, cw_ref, dtb_ref, alog_ref, dfull_ref, nw_ref, exp_ref, o_ref,
              state_ref, carry_ref, buf_ref, y_ref, *, d_inner):
    c = pl.program_id(1)
    L = zx_ref.shape[0]
    G, N, P = SSD_GROUPS, SSD_STATE, SSD_HEAD_DIM
    gw = d_inner // G
    conv_ch = d_inner + 2 * G * N

    @pl.when(c == 0)
    def _():
        state_ref[...] = jnp.zeros_like(state_ref)
        carry_ref[...] = jnp.zeros_like(carry_ref)

    u = zx_ref[:, d_inner:d_inner + conv_ch]
    buf_ref[0:SUBLANES, :] = carry_ref[...]
    buf_ref[SUBLANES:SUBLANES + L, :] = u
    carry_ref[...] = u[L - SUBLANES:L, :]
    cw = cw_ref[...]
    conv = cw[SSD_CONV:SSD_CONV + 1, :] + cw[SSD_CONV - 1:SSD_CONV, :] * u
    win = buf_ref[...]
    for k in range(SSD_CONV - 1):
        conv = conv + cw[k:k + 1, :] * pltpu.roll(win, SSD_CONV - 1 - k, 0)[SUBLANES:, :]
    xbc = _silu(conv)
    xs = xbc[:, :d_inner]
    bm = xbc[:, d_inner:d_inner + G * N]
    cm = xbc[:, d_inner + G * N:]

    dt = _softplus(dt_ref[...] + dtb_ref[...])
    a = dt * (-jnp.exp(alog_ref[...]))
    r_i = lax.broadcasted_iota(jnp.int32, (L, L), 0)
    c_i = lax.broadcasted_iota(jnp.int32, (L, L), 1)
    causal = c_i <= r_i
    acum_t = _dot_split(a.T, jnp.where(c_i >= r_i, 1.0, 0.0).astype(BF16))
    acum = acum_t.T
    expand = exp_ref[...]
    dt_full = _dot_split(dt, expand)
    acum_full = _dot_split(acum, expand)
    last_full = acum_full[L - 1:L, :]
    xdt = xs * dt_full
    xdt_bf = xdt.astype(BF16)
    xdec_bf = (xdt * jnp.exp(last_full - acum_full)).astype(BF16)
    grow = jnp.exp(acum_full)
    chunk_decay = jnp.exp(last_full)
    lane = lax.broadcasted_iota(jnp.int32, (L, LANES), 1)
    first_half = lane < P

    for g in range(G):
        b_g = bm[:, g * N:(g + 1) * N]
        c_g = cm[:, g * N:(g + 1) * N].astype(BF16)
        cb = _dot_nt(c_g, b_g.astype(BF16))
        prev = state_ref[g]
        gs = slice(g * gw, (g + 1) * gw)
        y_off = _dot(c_g, prev.astype(BF16)) * grow[:, gs]
        parts = []
        for pair in range(gw // LANES):
            h0 = g * (gw // P) + 2 * pair
            xp = xdt_bf[:, h0 * P:h0 * P + LANES]
            ms = []
            for h in (h0, h0 + 1):
                seg = acum[:, h:h + 1] - acum_t[h:h + 1, :]
                ms.append((cb * jnp.exp(jnp.where(causal, seg, -jnp.inf))).astype(BF16))
            zero = jnp.zeros_like(xp)
            parts.append(_dot(ms[0], jnp.where(first_half, xp, zero)) + _dot(ms[1], jnp.where(first_half, zero, xp)))
        y_ref[:, gs] = jnp.concatenate(parts, axis=1) + y_off
        state_ref[g] = prev * chunk_decay[:, gs] + _dot(b_g.T.astype(BF16), xdec_bf[:, gs])

    y = (y_ref[...] + dfull_ref[...] * xs) * _silu(zx_ref[:, :d_inner])
    nw = nw_ref[...]
    for g in range(G):
        gs = slice(g * gw, (g + 1) * gw)
        o_ref[:, gs] = _rms_rows(y[:, gs], nw[:, gs]).astype(o_ref.dtype)


def _ssd_core(zx, dt_raw, cw, dt_bias, a_log, d_full, norm_w, expand, batch, seq_len, chunk=128):
    T = zx.shape[0]
    d_inner = d_full.shape[1]
    conv_ch = cw.shape[1]
    nc = seq_len // chunk
    G, N = SSD_GROUPS, SSD_STATE
    row = lambda n: pl.BlockSpec((1, n), lambda b, c: (0, 0))
    return pl.pallas_call(
        functools.partial(_ssd_body, d_inner=d_inner),
        grid=(batch, nc),
        in_specs=[
            pl.BlockSpec((chunk, zx.shape[1]), lambda b, c: (b * nc + c, 0)),
            pl.BlockSpec((chunk, LANES), lambda b, c: (b * nc + c, 0)),
            pl.BlockSpec((SUBLANES, conv_ch), lambda b, c: (0, 0)),
            row(LANES), row(LANES), row(d_inner), row(d_inner),
            pl.BlockSpec((LANES, d_inner), lambda b, c: (0, 0)),
        ],
        out_specs=pl.BlockSpec((chunk, d_inner), lambda b, c: (b * nc + c, 0)),
        out_shape=jax.ShapeDtypeStruct((T, d_inner), BF16),
        scratch_shapes=[
            pltpu.VMEM((G, N, d_inner // G), F32),
            pltpu.VMEM((SUBLANES, conv_ch), F32),
            pltpu.VMEM((SUBLANES + chunk, conv_ch), F32),
            pltpu.VMEM((chunk, d_inner), F32),
        ],
        compiler_params=_cparams("parallel", "arbitrary"),
        name="ssd_core",
    )(zx, dt_raw, cw, dt_bias, a_log, d_full, norm_w, expand)


def _ssd_layer(x, norm_w, w_in, conv_w, conv_b, dt_bias, a_log, d_skip, ssd_norm, w_out, batch, seq_len,
               ffn_w_up, ffn_w_down, layer):
    heads = d_skip.shape[0]
    d_inner = heads * SSD_HEAD_DIM
    n_zx = w_in.shape[1] - heads
    assert heads <= LANES
    pad = lambda v: jnp.concatenate([v.astype(F32), jnp.zeros((LANES - heads,), F32)]).reshape(1, LANES)
    w_dt = jnp.concatenate([w_in[:, n_zx:], jnp.zeros((w_in.shape[0], LANES - heads), F32)], axis=1).astype(BF16)
    dt_raw, h = _proj(x, w_dt, norm_w=norm_w, emit_norm=True, name="ssd_dt")
    zx, ffn_up = _proj(h, w_in.astype(BF16), cols=(0, n_zx), name="ssd_in",
                       casts=(_cast_spec(ffn_w_up, layer, 1, 64),))
    expand = (jnp.arange(LANES)[:, None] == (jnp.arange(d_inner) // SSD_HEAD_DIM)[None, :]).astype(BF16)
    d_full = jnp.repeat(d_skip.astype(F32), SSD_HEAD_DIM).reshape(1, d_inner)
    y = _ssd_core(zx, dt_raw, _conv_rows(conv_w, conv_b), pad(dt_bias), pad(a_log), d_full,
                  ssd_norm.reshape(1, d_inner).astype(F32), expand, batch, seq_len)
    x, ffn_down = _proj(y, w_out.astype(BF16), res=x, name="ssd_out", casts=(_cast_spec(ffn_w_down, layer, 0, 32),))
    return x, ffn_up, ffn_down


def _ret_body(qk_ref, v_ref, g_ref, o_ref, state_ref, decay_ref, *, heads, dk, dv):
    c = pl.program_id(1)
    L = qk_ref.shape[0]

    @pl.when(c == 0)
    def _():
        state_ref[...] = jnp.zeros_like(state_ref)
        r_i = lax.broadcasted_iota(jnp.int32, (L, L), 0)
        c_i = lax.broadcasted_iota(jnp.int32, (L, L), 1)
        diff = (r_i - c_i).astype(F32)
        for h in range(heads):
            log_g = math.log(1.0 - 2.0 ** (-5.0 - h))
            decay_ref[h] = jnp.where(diff >= 0, jnp.exp(jnp.maximum(diff, 0.0) * log_g), 0.0)

    idx = lax.broadcasted_iota(jnp.int32, (L, 1), 0).astype(F32)

    for h in range(heads):
        log_g = math.log(1.0 - 2.0 ** (-5.0 - h))
        d_intra = decay_ref[h]
        k_dec = jnp.exp((L - 1.0 - idx) * log_g)
        q_dec = jnp.exp((idx + 1.0) * log_g)
        q = qk_ref[:, h * dk:(h + 1) * dk]
        k = qk_ref[:, (heads + h) * dk:(heads + h + 1) * dk]
        vs = slice(h * dv, (h + 1) * dv)
        v = v_ref[:, vs]
        prev = state_ref[h]
        scores = (_dot_nt(q, k) * d_intra).astype(BF16)
        o = _dot(scores, v) + q_dec * _dot(q, prev.astype(BF16))
        kd_t = (k.astype(F32) * k_dec).T.astype(BF16)
        state_ref[h] = math.exp(L * log_g) * prev + _dot(kd_t, v)
        ms = jnp.mean(o * o, axis=-1, keepdims=True)
        o_ref[:, vs] = (_silu(g_ref[:, vs]) * (o * lax.rsqrt(ms + RMS_EPS))).astype(o_ref.dtype)


def _ret_core(qk, v, g, batch, seq_len, dk, dv, chunk=256):
    T = qk.shape[0]
    H = RET_HEADS
    nc = seq_len // chunk
    blk = lambda n: pl.BlockSpec((chunk, n), lambda b, c: (b * nc + c, 0))
    return pl.pallas_call(
        functools.partial(_ret_body, heads=H, dk=dk, dv=dv),
        grid=(batch, nc),
        in_specs=[blk(2 * H * dk), blk(H * dv), blk(H * dv)],
        out_specs=blk(H * dv),
        out_shape=jax.ShapeDtypeStruct((T, H * dv), BF16),
        scratch_shapes=[pltpu.VMEM((H, dk, dv), F32), pltpu.VMEM((H, chunk, chunk), F32)],
        compiler_params=_cparams("parallel", "arbitrary"),
        name="ret_core",
    )(qk, v, g)


def _ret_layer(x, norm_w, tabs, w_in, wo, batch, seq_len, ffn_w_up, ffn_w_down, layer):
    cos_r, sin_r = tabs[2], tabs[3]
    D = x.shape[1]
    H = RET_HEADS
    dk = D // H
    dv = 2 * D // H
    n_qk = 2 * H * dk
    n_v = H * dv
    n_all = w_in.shape[1]
    col = jnp.arange(n_all)
    col_scale = jnp.where((col >= H * dk) & (col < n_qk), dk ** -0.5, 1.0).astype(F32)
    w = (w_in * col_scale[None, :]).astype(BF16)
    g, h = _proj(x, w, cols=(n_qk + n_v, n_all - n_qk - n_v), norm_w=norm_w, emit_norm=True, name="ret_g")
    qk, ffn_up = _proj(h, w, cols=(0, n_qk), epi="rope_half", cos=cos_r, sin=sin_r, out_dtype=BF16,
                       casts=(_cast_spec(ffn_w_up, layer, 1, 32),), name="ret_qk")
    v, ffn_down = _proj(h, w, cols=(n_qk, n_v), out_dtype=BF16,
                        casts=(_cast_spec(ffn_w_down, layer, 0, 32),), name="ret_v")
    o = _ret_core(qk, v, g, batch, seq_len, dk, dv)
    return _proj(o, wo.astype(BF16), res=x, name="ret_out"), ffn_up, ffn_down


def kernel(x, positions, norm_mix, norm_ffn, norm_final, mla_wq_a, mla_q_norm, mla_wq_b, mla_wkv_a, mla_kv_norm, mla_wkv_b, mla_wo, sb_wqkv, sb_wo, ssd_w_in, ssd_conv_w, ssd_conv_b, ssd_dt_bias, ssd_a_log, ssd_d, ssd_norm, ssd_w_out, ret_w_in, ret_wo, ffn_w_up, ffn_conv_w, ffn_conv_b, ffn_w_down):
    B, S, D = x.shape
    depth = norm_mix.shape[0]
    n_mixers = 4
    tabs = _rope_tables(positions, D // RET_HEADS)
    cw = _conv_rows(ffn_conv_w, ffn_conv_b)
    xt = x.reshape(B * S, D)
    for i in range(depth):
        m, j = i % n_mixers, i // n_mixers
        ffn = (ffn_w_up, ffn_w_down, i)
        if m == 0:
            xt, w_up, w_down = _mla_layer(xt, norm_mix[i], tabs, mla_wq_a[j], mla_q_norm[j], mla_wq_b[j],
                                          mla_wkv_a[j], mla_kv_norm[j], mla_wkv_b[j], mla_wo[j], B, S, *ffn)
        elif m == 1:
            xt, w_up, w_down = _sb_layer(xt, norm_mix[i], sb_wqkv[j], sb_wo[j], B, S, *ffn)
        elif m == 2:
            xt, w_up, w_down = _ssd_layer(xt, norm_mix[i], ssd_w_in[j], ssd_conv_w[j], ssd_conv_b[j],
                                          ssd_dt_bias[j], ssd_a_log[j], ssd_d[j], ssd_norm[j], ssd_w_out[j],
                                          B, S, *ffn)
        else:
            xt, w_up, w_down = _ret_layer(xt, norm_mix[i], tabs, ret_w_in[j], ret_wo[j], B, S, *ffn)
        xt = _ffn(xt, norm_ffn[i], w_up, cw, w_down, i, S)
    return _rmsnorm(xt, norm_final).reshape(B, S, D)
```
